```python
import jax, jax.numpy as jnp
from jax import lax
import numpy as np

D_MODEL = 1024
BATCH = 4
SEQ = 4096
DEPTH = 2

GRID_W = 64
CTX_LEN = 256
D_MIX = D_MODEL
HEAD_DIM = 64
A_WIDTH = D_MIX // 4
A_GROUPS = A_WIDTH // HEAD_DIM
CHUNK = 128
B_WIDTH = D_MIX // 4
POOL_WINDOWS = (2, 4, 8, 16)
B_GROUPS = len(POOL_WINDOWS)
B_GROUP_DIM = B_WIDTH // B_GROUPS
C_WIDTH = D_MIX - A_WIDTH - B_WIDTH
C_HEADS = C_WIDTH // HEAD_DIM
NA_ROWS_MAX = 8
NA_COLS = 16
ATTN_SCALE = HEAD_DIM ** -0.5
AV_OFF = A_WIDTH
B_OFF = 2 * A_WIDTH
CQ_OFF = B_OFF + B_WIDTH
CK_OFF = CQ_OFF + C_WIDTH
CV_OFF = CK_OFF + C_WIDTH
IN_COLS = CV_OFF + C_WIDTH
IN_SPLITS = [AV_OFF, B_OFF, CQ_OFF, CK_OFF, CV_OFF]
D_FF = 2816
N_EXPERTS = 8
TOP_K = 2
D_FF_EXPERT = 3584
N_DENSE = (DEPTH + 1) // 2
N_MOE = DEPTH // 2
EPS = 1e-6
NEG_INF = -1e30

kernel_name = 'hybrid_pool_sgu_natten_moe_dit'


def rms_norm(x, g):
    xf = x.astype(jnp.float32)
    y = xf * lax.rsqrt(jnp.mean(xf * xf, axis=-1, keepdims=True) + EPS)
    return (y * g.astype(jnp.float32)).astype(x.dtype)


def layer_norm(x, g, b):
    xf = x.astype(jnp.float32)
    xc = xf - jnp.mean(xf, axis=-1, keepdims=True)
    var = jnp.mean(xc * xc, axis=-1, keepdims=True)
    y = xc * lax.rsqrt(var + EPS) * g.astype(jnp.float32) + b.astype(jnp.float32)
    return y.astype(x.dtype)


def modulation(cond, w_mod, b_mod):
    m = jax.nn.silu(cond) @ w_mod + b_mod
    return jnp.split(m, 6, axis=-1)


def split_heads(t):
    return t.reshape(t.shape[:-1] + (C_HEADS, HEAD_DIM))


def chunk_sgu(u, v, vn_g, vn_b, w_s, b_s):
    bsz, n_tok, _ = u.shape
    n_chunks = n_tok // CHUNK
    u = jax.nn.gelu(u, approximate=False)
    v = layer_norm(jax.nn.gelu(v, approximate=False), vn_g, vn_b)
    v = v.reshape(bsz, n_chunks, CHUNK, A_GROUPS, HEAD_DIM)
    mixed = jnp.einsum('gij,bnjgd->bnigd', w_s, v) + b_s.T[:, :, None]
    return u * mixed.reshape(bsz, n_tok, A_WIDTH)


def multiscale_pool(z, w_pool, scale):
    bsz, n_tok, _ = z.shape
    zf = z.reshape(bsz, n_tok, B_GROUPS, B_GROUP_DIM).astype(jnp.float32)
    csum = jnp.pad(jnp.cumsum(zf, axis=1), ((0, 0), (1, 0), (0, 0), (0, 0)))
    pos = jnp.arange(n_tok)
    means = []
    for gi, win in enumerate(POOL_WINDOWS):
        lo = jnp.clip(pos - win // 2, 0, n_tok)
        hi = jnp.clip(pos + win // 2, 0, n_tok)
        cnt = (hi - lo).astype(jnp.float32)
        cg = csum[:, :, gi]
        means.append((cg[:, hi] - cg[:, lo]) / cnt[None, :, None])
    pooled = jnp.stack(means, axis=2)
    y = (pooled - zf).astype(z.dtype)
    y = jnp.einsum('blgd,gde->blge', y, w_pool)
    return y.reshape(bsz, n_tok, B_WIDTH) * scale


def neighbourhood_attention(q, k, v, k_ctx, v_ctx, rpb):
    bsz, n_tok = q.shape[:2]
    rows = n_tok // GRID_W
    kh = min(NA_ROWS_MAX, rows)
    r = jnp.arange(rows)
    row_idx = jnp.clip(r - kh // 2, 0, rows - kh)[:, None] + jnp.arange(kh)[None, :]
    row_off = row_idx - r[:, None] + (NA_ROWS_MAX - 1)
    col = jnp.arange(GRID_W)
    col_start = jnp.clip(col - NA_COLS // 2, 0, GRID_W - NA_COLS)
    col_in = (col[None, :] >= col_start[:, None]) & (col[None, :] < col_start[:, None] + NA_COLS)
    col_off = jnp.clip(col[None, :] - col[:, None], -(NA_COLS - 1), NA_COLS - 1) + (NA_COLS - 1)
    bias = rpb[:, row_off[:, None, :, None], col_off[None, :, None, :]].astype(jnp.float32)
    bias = jnp.where(col_in[None, None, :, None, :], bias, NEG_INF)
    bias = jnp.transpose(bias, (1, 0, 2, 3, 4))
    grid_shape = (bsz, rows, GRID_W, C_HEADS, HEAD_DIM)
    qg = q.reshape(grid_shape)
    kg = jnp.take(k.reshape(grid_shape), row_idx, axis=1)
    vg = jnp.take(v.reshape(grid_shape), row_idx, axis=1)
    n_loc = kh * GRID_W
    s_loc = jnp.einsum('brqhd,brikhd->brhqik', qg, kg).astype(jnp.float32) * ATTN_SCALE + bias
    s_ctx = jnp.einsum('brqhd,bchd->brhqc', qg, k_ctx).astype(jnp.float32) * ATTN_SCALE
    s = jnp.concatenate([s_loc.reshape(bsz, rows, C_HEADS, GRID_W, n_loc), s_ctx], axis=-1)
    p = jax.nn.softmax(s, axis=-1).astype(v.dtype)
    p_loc = p[..., :n_loc].reshape(bsz, rows, C_HEADS, GRID_W, kh, GRID_W)
    o = (jnp.einsum('brhqik,brikhd->brqhd', p_loc, vg)
         + jnp.einsum('brhqc,bchd->brqhd', p[..., n_loc:], v_ctx))
    return o.reshape(bsz, n_tok, C_WIDTH)


def context_attention(q_ctx, k_ctx, v_ctx):
    bsz, n_ctx = q_ctx.shape[:2]
    s = jnp.einsum('bqhd,bkhd->bhqk', q_ctx, k_ctx).astype(jnp.float32) * ATTN_SCALE
    p = jax.nn.softmax(s, axis=-1).astype(v_ctx.dtype)
    return jnp.einsum('bhqk,bkhd->bqhd', p, v_ctx).reshape(bsz, n_ctx, C_WIDTH)


def swiglu(h, w_gate, w_up, w_down):
    return (jax.nn.silu(h @ w_gate) * (h @ w_up)) @ w_down


def moe_swiglu(h, w_router, w_gate, w_up, w_down):
    logits = (h @ w_router).astype(jnp.float32)
    top_vals, top_idx = lax.top_k(logits, TOP_K)
    gates = jax.nn.softmax(top_vals, axis=-1)
    dense_gates = jnp.sum(jax.nn.one_hot(top_idx, N_EXPERTS, dtype=jnp.float32) * gates[..., None], axis=-2)
    dense_gates = dense_gates.astype(h.dtype)
    y = jnp.zeros_like(h)
    for e in range(N_EXPERTS):
        y = y + dense_gates[..., e:e + 1] * swiglu(h, w_gate[e], w_up[e], w_down[e])
    return y


def setup_inputs(seed: int = 0) -> dict:
    key = jax.random.key(seed)
    ks = jax.random.split(key, 26)
    f32 = jnp.float32

    def nrm(k, shape, s):
        return jax.random.normal(k, shape, f32) * s

    L = DEPTH
    return {
        'x': nrm(ks[0], (BATCH, SEQ, D_MODEL), 1.0),
        'c': nrm(ks[1], (BATCH, D_MODEL), 1.0),
        'ctx': nrm(ks[2], (BATCH, CTX_LEN, D_MODEL), 1.0),
        'c_ctx': nrm(ks[3], (D_MODEL,), 1.0),
        'w_mod': nrm(ks[4], (L, D_MODEL, 6 * D_MODEL), 0.5 * D_MODEL ** -0.5),
        'b_mod': nrm(ks[5], (L, 6 * D_MODEL), 0.02),
        'norm1_g': 1.0 + nrm(ks[6], (L, D_MODEL), 0.01),
        'norm2_g': 1.0 + nrm(ks[7], (L, D_MODEL), 0.01),
        'w_in': nrm(ks[8], (L, D_MODEL, IN_COLS), D_MODEL ** -0.5),
        'w_out': nrm(ks[9], (L, D_MIX, D_MODEL), D_MIX ** -0.5),
        'a_vn_g': 1.0 + nrm(ks[10], (L, A_WIDTH), 0.01),
        'a_vn_b': nrm(ks[11], (L, A_WIDTH), 0.01),
        'a_ws': nrm(ks[12], (L, A_GROUPS, CHUNK, CHUNK), CHUNK ** -0.5),
        'a_bs': 1.0 + nrm(ks[13], (L, A_GROUPS, CHUNK), 0.01),
        'b_wpool': nrm(ks[14], (L, B_GROUPS, B_GROUP_DIM, B_GROUP_DIM), B_GROUP_DIM ** -0.5),
        'b_scale': 1.0 + nrm(ks[15], (L, B_WIDTH), 0.01),
        'c_qn_g': 1.0 + nrm(ks[16], (L, HEAD_DIM), 0.01),
        'c_kn_g': 1.0 + nrm(ks[17], (L, HEAD_DIM), 0.01),
        'c_rpb': nrm(ks[18], (L, C_HEADS, 2 * NA_ROWS_MAX - 1, 2 * NA_COLS - 1), 0.02),
        'ffn_w_gate': nrm(ks[19], (N_DENSE, D_MODEL, D_FF), D_MODEL ** -0.5),
        'ffn_w_up': nrm(ks[20], (N_DENSE, D_MODEL, D_FF), D_MODEL ** -0.5),
        'ffn_w_down': nrm(ks[21], (N_DENSE, D_FF, D_MODEL), D_FF ** -0.5),
        'moe_w_router': nrm(ks[22], (N_MOE, D_MODEL, N_EXPERTS), D_MODEL ** -0.5),
        'moe_w_gate': nrm(ks[23], (N_MOE, N_EXPERTS, D_MODEL, D_FF_EXPERT), D_MODEL ** -0.5),
        'moe_w_up': nrm(ks[24], (N_MOE, N_EXPERTS, D_MODEL, D_FF_EXPERT), D_MODEL ** -0.5),
        'moe_w_down': nrm(ks[25], (N_MOE, N_EXPERTS, D_FF_EXPERT, D_MODEL), D_FF_EXPERT ** -0.5),
    }


def reference(x, c, ctx, c_ctx, w_mod, b_mod, norm1_g, norm2_g, w_in, w_out,
              a_vn_g, a_vn_b, a_ws, a_bs, b_wpool, b_scale, c_qn_g, c_kn_g, c_rpb,
              ffn_w_gate, ffn_w_up, ffn_w_down,
              moe_w_router, moe_w_gate, moe_w_up, moe_w_down):
    cond_x = c[:, None, :]
    cond_c = c_ctx[None, None, :]
    xc = ctx
    for l in range(DEPTH):
        last = l == DEPTH - 1
        sh1, sc1, ga1, sh2, sc2, ga2 = modulation(cond_x, w_mod[l], b_mod[l])
        csh1, csc1, cga1, csh2, csc2, cga2 = modulation(cond_c, w_mod[l], b_mod[l])
        fi = l // 2

        def channel_mixer(t):
            if l % 2 == 0:
                return swiglu(t, ffn_w_gate[fi], ffn_w_up[fi], ffn_w_down[fi])
            return moe_swiglu(t, moe_w_router[fi], moe_w_gate[fi], moe_w_up[fi], moe_w_down[fi])

        hc = rms_norm(xc, norm1_g[l]) * (1 + csc1) + csh1
        if last:
            kv_c = hc @ w_in[l][:, CK_OFF:]
        else:
            proj_c = hc @ w_in[l]
            kv_c = proj_c[..., CK_OFF:]
        k_c = rms_norm(split_heads(kv_c[..., :C_WIDTH]), c_kn_g[l])
        v_c = split_heads(kv_c[..., C_WIDTH:])

        h = rms_norm(x, norm1_g[l]) * (1 + sc1) + sh1
        u, v, zb, q, k, vv = jnp.split(h @ w_in[l], IN_SPLITS, axis=-1)
        q = rms_norm(split_heads(q), c_qn_g[l])
        k = rms_norm(split_heads(k), c_kn_g[l])
        vv = split_heads(vv)
        mix = jnp.concatenate([
            chunk_sgu(u, v, a_vn_g[l], a_vn_b[l], a_ws[l], a_bs[l]),
            multiscale_pool(zb, b_wpool[l], b_scale[l]),
            neighbourhood_attention(q, k, vv, k_c, v_c, c_rpb[l]),
        ], axis=-1)
        x_new = x + ga1 * (mix @ w_out[l])
        h2 = rms_norm(x_new, norm2_g[l]) * (1 + sc2) + sh2
        x_new = x_new + ga2 * channel_mixer(h2)

        if not last:
            uc, vcm, zc, qc = jnp.split(proj_c[..., :CK_OFF], [AV_OFF, B_OFF, CQ_OFF], axis=-1)
            qc = rms_norm(split_heads(qc), c_qn_g[l])
            mix_c = jnp.concatenate([
                chunk_sgu(uc, vcm, a_vn_g[l], a_vn_b[l], a_ws[l], a_bs[l]),
                multiscale_pool(zc, b_wpool[l], b_scale[l]),
                context_attention(qc, k_c, v_c),
            ], axis=-1)
            xc = xc + cga1 * (mix_c @ w_out[l])
            h2c = rms_norm(xc, norm2_g[l]) * (1 + csc2) + csh2
            xc = xc + cga2 * channel_mixer(h2c)
        x = x_new
    return x
```

```python
import functools

import numpy as np
import jax
import jax.numpy as jnp
from jax import lax
from jax.experimental import pallas as pl
from jax.experimental.pallas import tpu as pltpu

F32 = jnp.float32
BF16 = jnp.bfloat16

D_MODEL = 1024
GRID_W = 64
HEAD_DIM = 64
A_WIDTH = 256
A_GROUPS = 4
CHUNK = 128
B_WIDTH = 256
POOL_WINDOWS = (2, 4, 8, 16)
POOL_HALO = 8
C_WIDTH = 512
C_HEADS = 8
NA_ROWS = 8
NA_COLS = 16
ATTN_SCALE = HEAD_DIM ** -0.5
IN_COLS = 2 * A_WIDTH + B_WIDTH + 3 * C_WIDTH
N_EXPERTS = 8
EPS = 1e-6
NEG_INF = -1e30

ROW_BLOCK = 4
KEY_ROWS = ROW_BLOCK + NA_ROWS
MOD_ROWS = 8
CTX_MOD_ROW = 4
GATE_LANES = 128

VMEM_LIMIT = 60 * 1024 * 1024


def _cparams(sem):
    return pltpu.CompilerParams(dimension_semantics=sem, vmem_limit_bytes=VMEM_LIMIT)


def _dot(a, b):
    return jnp.dot(a, b, preferred_element_type=F32)


def _dot_nt(a, b):
    return lax.dot_general(a, b, (((1,), (1,)), ((), ())), preferred_element_type=F32)


def _silu(t):
    return t / (1.0 + jnp.exp(-t))


def _gelu(t):
    return 0.5 * t * (1.0 + lax.erf(t * np.float32(np.sqrt(0.5))))


def _norm_mod(x, g, scale, shift):
    y = x * lax.rsqrt(jnp.mean(x * x, axis=-1, keepdims=True) + EPS) * g
    return y * (1.0 + scale) + shift


def _mod_kernel(cond_ref, w_ref, b_ref, o_ref):
    s = _silu(cond_ref[...])
    o_ref[...] = _dot(s.astype(BF16), w_ref[...].astype(BF16)) + b_ref[...]


def _modulation(cond, w_mod, b_mod):
    depth, d, cols = w_mod.shape
    tn = 1536
    return pl.pallas_call(
        _mod_kernel,
        grid=(depth, cols // tn),
        in_specs=[
            pl.BlockSpec((MOD_ROWS, d), lambda l, j: (0, 0)),
            pl.BlockSpec((None, d, tn), lambda l, j: (l, 0, j)),
            pl.BlockSpec((None, 1, tn), lambda l, j: (l, 0, j)),
        ],
        out_specs=pl.BlockSpec((None, MOD_ROWS, tn), lambda l, j: (l, 0, j)),
        out_shape=jax.ShapeDtypeStruct((depth, MOD_ROWS, cols), F32),
        compiler_params=_cparams(("arbitrary", "arbitrary")),
        name="modulation",
    )(cond, w_mod, b_mod.reshape(depth, 1, cols))


def _head_rms(t, hsum):
    tt = t * t
    hi = tt.astype(BF16)
    lo = (tt - hi.astype(F32)).astype(BF16)
    ms = _dot(hi, hsum) + _dot(lo, hsum)
    return t * lax.rsqrt(ms + EPS)


def _in_proj_kernel(x_ref, mod_ref, g_ref, w_ref, hs_ref, qg_ref, kg_ref,
                    uv_ref, z_ref, q_ref, k_ref, v_ref):
    h = _norm_mod(x_ref[...], g_ref[...], mod_ref[1:2, :], mod_ref[0:1, :])
    p = _dot(h.astype(BF16), w_ref[...])
    o = 2 * A_WIDTH
    uv_ref[...] = p[:, :o].astype(BF16)
    z_ref[...] = p[:, o:o + B_WIDTH]
    o += B_WIDTH
    hs = hs_ref[...]
    q_ref[...] = (_head_rms(p[:, o:o + C_WIDTH], hs) * qg_ref[...]).astype(BF16)
    o += C_WIDTH
    k_ref[...] = (_head_rms(p[:, o:o + C_WIDTH], hs) * kg_ref[...]).astype(BF16)
    o += C_WIDTH
    v_ref[...] = p[:, o:o + C_WIDTH].astype(BF16)


def _in_proj(x, mods_l, g1, w_in, hsum, qg, kg, *, is_ctx, tm):
    bsz, n_tok, d = x.shape
    if is_ctx:
        mod_map = lambda b, i: (CTX_MOD_ROW, 0, 0)
    else:
        mod_map = lambda b, i: (b, 0, 0)
    const = lambda b, i: (0, 0)
    tok = lambda b, i: (b, i, 0)

    def out(width, dtype):
        return (pl.BlockSpec((None, tm, width), tok),
                jax.ShapeDtypeStruct((bsz, n_tok, width), dtype))

    outs = [out(2 * A_WIDTH, BF16), out(B_WIDTH, F32), out(C_WIDTH, BF16),
            out(C_WIDTH, BF16), out(C_WIDTH, BF16)]
    return pl.pallas_call(
        _in_proj_kernel,
        grid=(bsz, n_tok // tm),
        in_specs=[
            pl.BlockSpec((None, tm, d), tok),
            pl.BlockSpec((None, 6, d), mod_map),
            pl.BlockSpec((1, d), const),
            pl.BlockSpec((d, IN_COLS), const),
            pl.BlockSpec((C_WIDTH, C_WIDTH), const),
            pl.BlockSpec((1, C_WIDTH), const),
            pl.BlockSpec((1, C_WIDTH), const),
        ],
        out_specs=[o[0] for o in outs],
        out_shape=[o[1] for o in outs],
        compiler_params=_cparams(("arbitrary", "arbitrary")),
        name="in_proj_ctx" if is_ctx else "in_proj",
    )(x, mods_l, g1, w_in, hsum, qg, kg)


def _mixer_a(uv, vn_g, vn_b, ws_ref, bs_full):
    n_tok = uv.shape[0]
    u = _gelu(uv[:, :A_WIDTH])
    v = _gelu(uv[:, A_WIDTH:])
    vc = v - jnp.mean(v, axis=-1, keepdims=True)
    var = jnp.mean(vc * vc, axis=-1, keepdims=True)
    v = (vc * lax.rsqrt(var + EPS) * vn_g + vn_b).astype(BF16)
    lane_group = lax.broadcasted_iota(jnp.int32, (CHUNK, A_WIDTH), 1) // HEAD_DIM
    outs = []
    for c in range(n_tok // CHUNK):
        v_c = v[c * CHUNK:(c + 1) * CHUNK]
        mixed = bs_full
        for g in range(A_GROUPS):
            mixed = mixed + jnp.where(lane_group == g, _dot(ws_ref[g], v_c), 0.0)
        outs.append(u[c * CHUNK:(c + 1) * CHUNK] * mixed)
    return jnp.concatenate(outs, axis=0) if len(outs) > 1 else outs[0]


def _mixer_b(zh_ref, t0, n_tok, seq_len, wpool_bd, b_scale):
    def sh(d):
        return zh_ref[POOL_HALO + d:POOL_HALO + d + n_tok, :]

    z = sh(0)
    s = sh(-1) + z
    sums = [s]
    for win in POOL_WINDOWS[1:]:
        half = win // 2
        for d in range(-half, -half // 2):
            s = s + sh(d)
        for d in range(half // 2, half):
            s = s + sh(d)
        sums.append(s)
    lane_group = lax.broadcasted_iota(jnp.int32, (n_tok, B_WIDTH), 1) // (B_WIDTH // len(POOL_WINDOWS))
    pos = t0 + lax.broadcasted_iota(jnp.int32, (n_tok, B_WIDTH), 0)
    half = jnp.left_shift(1, lane_group)
    cnt = jnp.minimum(pos + half, seq_len) - jnp.maximum(pos - half, 0)
    total = sums[0]
    for g in range(1, len(POOL_WINDOWS)):
        total = jnp.where(lane_group == g, sums[g], total)
    y = total / cnt.astype(F32) - z
    return _dot(y.astype(BF16), wpool_bd) * b_scale


def _fill_halo(zh_ref, z_ref, t0, n_tok, seq_len):
    zh_ref[POOL_HALO:POOL_HALO + n_tok, :] = z_ref[pl.ds(t0, n_tok), :]
    lo = jnp.maximum(t0 - POOL_HALO, 0)
    hi = jnp.minimum(t0 + n_tok, seq_len - POOL_HALO)
    lo = pl.multiple_of(lo, POOL_HALO)
    hi = pl.multiple_of(hi, POOL_HALO)
    before = z_ref[pl.ds(lo, POOL_HALO), :]
    after = z_ref[pl.ds(hi, POOL_HALO), :]
    zh_ref[0:POOL_HALO, :] = jnp.where(t0 > 0, before, 0.0)
    zh_ref[POOL_HALO + n_tok:, :] = jnp.where(t0 + n_tok < seq_len, after, 0.0)


def _attend(q_h, k_h, v_h, bias_h, kc_h, vc_h):
    s = _dot_nt(q_h, k_h)
    if bias_h is not None:
        s = s + bias_h
    m = jnp.max(s, axis=-1, keepdims=True)
    if kc_h is not None:
        sc = _dot_nt(q_h, kc_h)
        m = jnp.maximum(m, jnp.max(sc, axis=-1, keepdims=True))
        pc = jnp.exp(sc - m)
    p = jnp.exp(s - m)
    den = jnp.sum(p, axis=-1, keepdims=True)
    o = _dot(p.astype(BF16), v_h)
    if kc_h is not None:
        den = den + jnp.sum(pc, axis=-1, keepdims=True)
        o = o + _dot(pc.astype(BF16), vc_h)
    return o / den


def _mix_kernel(x_ref, uv_ref, z_ref, q_ref, k_ref, v_ref, kc_ref, vc_ref, bias_ref, mod_ref,
                vn_g_ref, vn_b_ref, ws_ref, bs_ref, wpool_ref, bscale_ref, wout_ref,
                o_ref, zh_ref, mix_ref, *, seq_len):
    j = pl.program_id(1)
    n_tok = ROW_BLOCK * GRID_W
    t0 = pl.multiple_of(j * n_tok, n_tok)
    rows = seq_len // GRID_W

    a = _mixer_a(uv_ref[...].astype(F32), vn_g_ref[...], vn_b_ref[...], ws_ref, bs_ref[...])
    mix_ref[:, :A_WIDTH] = a.astype(BF16)

    _fill_halo(zh_ref, z_ref, t0, n_tok, seq_len)
    bmix = _mixer_b(zh_ref, t0, n_tok, seq_len, wpool_ref[...], bscale_ref[...])
    mix_ref[:, A_WIDTH:A_WIDTH + B_WIDTH] = bmix.astype(BF16)

    win_row = jnp.clip(j * ROW_BLOCK - NA_ROWS // 2, 0, rows - KEY_ROWS)
    koff = pl.multiple_of(win_row * GRID_W, GRID_W)
    c_off = A_WIDTH + B_WIDTH
    for h in range(C_HEADS):
        hs = slice(h * HEAD_DIM, (h + 1) * HEAD_DIM)
        o = _attend(q_ref[:, hs],
                    k_ref[pl.ds(koff, KEY_ROWS * GRID_W), hs],
                    v_ref[pl.ds(koff, KEY_ROWS * GRID_W), hs],
                    bias_ref[h], kc_ref[:, hs], vc_ref[:, hs])
        mix_ref[:, c_off + h * HEAD_DIM:c_off + (h + 1) * HEAD_DIM] = o.astype(BF16)

    y = _dot(mix_ref[...], wout_ref[...])
    o_ref[...] = x_ref[...] + mod_ref[2:3, :] * y


def _mix(x, uv, z, q, k, v, kc, vc, bias, mods_l, vn_g, vn_b, ws, bs_full, wpool_bd, b_scale, w_out):
    bsz, seq_len, d = x.shape
    n_ctx = kc.shape[1]
    n_tok = ROW_BLOCK * GRID_W
    n_blocks = seq_len // n_tok
    tok = lambda b, j: (b, j, 0)
    full = lambda b, j: (b, 0, 0)
    c2 = lambda b, j: (0, 0)
    c3 = lambda b, j: (0, 0, 0)

    def bias_map(b, j):
        kind = jnp.where(j == 0, 0, jnp.where(j == n_blocks - 1, 2, 1))
        return (kind, 0, 0, 0)

    return pl.pallas_call(
        functools.partial(_mix_kernel, seq_len=seq_len),
        grid=(bsz, n_blocks),
        in_specs=[
            pl.BlockSpec((None, n_tok, d), tok),
            pl.BlockSpec((None, n_tok, 2 * A_WIDTH), tok),
            pl.BlockSpec((None, seq_len, B_WIDTH), full),
            pl.BlockSpec((None, n_tok, C_WIDTH), tok),
            pl.BlockSpec((None, seq_len, C_WIDTH), full),
            pl.BlockSpec((None, seq_len, C_WIDTH), full),
            pl.BlockSpec((None, n_ctx, C_WIDTH), full),
            pl.BlockSpec((None, n_ctx, C_WIDTH), full),
            pl.BlockSpec((None, C_HEADS, n_tok, KEY_ROWS * GRID_W), bias_map),
            pl.BlockSpec((None, 6, d), full),
            pl.BlockSpec((1, A_WIDTH), c2),
            pl.BlockSpec((1, A_WIDTH), c2),
            pl.BlockSpec((A_GROUPS, CHUNK, CHUNK), c3),
            pl.BlockSpec((CHUNK, A_WIDTH), c2),
            pl.BlockSpec((B_WIDTH, B_WIDTH), c2),
            pl.BlockSpec((1, B_WIDTH), c2),
            pl.BlockSpec((d, d), c2),
        ],
        out_specs=pl.BlockSpec((None, n_tok, d), tok),
        out_shape=jax.ShapeDtypeStruct(x.shape, F32),
        scratch_shapes=[
            pltpu.VMEM((n_tok + 2 * POOL_HALO, B_WIDTH), F32),
            pltpu.VMEM((n_tok, d), BF16),
        ],
        compiler_params=_cparams(("arbitrary", "arbitrary")),
        name="mix",
    )(x, uv, z, q, k, v, kc, vc, bias, mods_l, vn_g, vn_b, ws, bs_full, wpool_bd, b_scale, w_out)


def _mix_ctx_kernel(x_ref, uv_ref, z_ref, q_ref, k_ref, v_ref, mod_ref,
                    vn_g_ref, vn_b_ref, ws_ref, bs_ref, wpool_ref, bscale_ref, wout_ref,
                    o_ref, zh_ref, mix_ref):
    n_tok = x_ref.shape[0]
    a = _mixer_a(uv_ref[...].astype(F32), vn_g_ref[...], vn_b_ref[...], ws_ref, bs_ref[...])
    mix_ref[:, :A_WIDTH] = a.astype(BF16)

    zero = jnp.zeros((POOL_HALO, B_WIDTH), F32)
    zh_ref[0:POOL_HALO, :] = zero
    zh_ref[POOL_HALO:POOL_HALO + n_tok, :] = z_ref[...]
    zh_ref[POOL_HALO + n_tok:, :] = zero
    bmix = _mixer_b(zh_ref, 0, n_tok, n_tok, wpool_ref[...], bscale_ref[...])
    mix_ref[:, A_WIDTH:A_WIDTH + B_WIDTH] = bmix.astype(BF16)

    c_off = A_WIDTH + B_WIDTH
    for h in range(C_HEADS):
        hs = slice(h * HEAD_DIM, (h + 1) * HEAD_DIM)
        o = _attend(q_ref[:, hs], k_ref[:, hs], v_ref[:, hs], None, None, None)
        mix_ref[:, c_off + h * HEAD_DIM:c_off + (h + 1) * HEAD_DIM] = o.astype(BF16)

    y = _dot(mix_ref[...], wout_ref[...])
    o_ref[...] = x_ref[...] + mod_ref[2:3, :] * y


def _mix_ctx(x, uv, z, q, k, v, mods_l, vn_g, vn_b, ws, bs_full, wpool_bd, b_scale, w_out):
    bsz, n_tok, d = x.shape
    full = lambda b: (b, 0, 0)
    c2 = lambda b: (0, 0)
    c3 = lambda b: (0, 0, 0)
    return pl.pallas_call(
        _mix_ctx_kernel,
        grid=(bsz,),
        in_specs=[
            pl.BlockSpec((None, n_tok, d), full),
            pl.BlockSpec((None, n_tok, 2 * A_WIDTH), full),
            pl.BlockSpec((None, n_tok, B_WIDTH), full),
            pl.BlockSpec((None, n_tok, C_WIDTH), full),
            pl.BlockSpec((None, n_tok, C_WIDTH), full),
            pl.BlockSpec((None, n_tok, C_WIDTH), full),
            pl.BlockSpec((None, 6, d), lambda b: (CTX_MOD_ROW, 0, 0)),
            pl.BlockSpec((1, A_WIDTH), c2),
            pl.BlockSpec((1, A_WIDTH), c2),
            pl.BlockSpec((A_GROUPS, CHUNK, CHUNK), c3),
            pl.BlockSpec((CHUNK, A_WIDTH), c2),
            pl.BlockSpec((B_WIDTH, B_WIDTH), c2),
            pl.BlockSpec((1, B_WIDTH), c2),
            pl.BlockSpec((d, d), c2),
        ],
        out_specs=pl.BlockSpec((None, n_tok, d), full),
        out_shape=jax.ShapeDtypeStruct(x.shape, F32),
        scratch_shapes=[
            pltpu.VMEM((n_tok + 2 * POOL_HALO, B_WIDTH), F32),
            pltpu.VMEM((n_tok, d), BF16),
        ],
        compiler_params=_cparams(("arbitrary",)),
        name="mix_ctx",
    )(x, uv, z, q, k, v, mods_l, vn_g, vn_b, ws, bs_full, wpool_bd, b_scale, w_out)


def _ffn_kernel(x_ref, mod_ref, g_ref, wg_ref, wu_ref, wd_ref, o_ref, h_ref, acc_ref):
    f = pl.program_id(2)

    @pl.when(f == 0)
    def _():
        h = _norm_mod(x_ref[...], g_ref[...], mod_ref[4:5, :], mod_ref[3:4, :])
        h_ref[...] = h.astype(BF16)
        acc_ref[...] = jnp.zeros_like(acc_ref)

    h = h_ref[...]
    act = _silu(_dot(h, wg_ref[...])) * _dot(h, wu_ref[...])
    acc_ref[...] += _dot(act.astype(BF16), wd_ref[...])

    @pl.when(f == pl.num_programs(2) - 1)
    def _():
        o_ref[...] = x_ref[...] + mod_ref[5:6, :] * acc_ref[...]


def _ffn(x, mods_l, g2, w_gate, w_up, w_down, *, is_ctx, tm, tf):
    bsz, n_tok, d = x.shape
    d_ff = w_gate.shape[1]
    if is_ctx:
        mod_map = lambda b, i, f: (CTX_MOD_ROW, 0, 0)
    else:
        mod_map = lambda b, i, f: (b, 0, 0)
    tok = lambda b, i, f: (b, i, 0)
    return pl.pallas_call(
        _ffn_kernel,
        grid=(bsz, n_tok // tm, d_ff // tf),
        in_specs=[
            pl.BlockSpec((None, tm, d), tok),
            pl.BlockSpec((None, 6, d), mod_map),
            pl.BlockSpec((1, d), lambda b, i, f: (0, 0)),
            pl.BlockSpec((d, tf), lambda b, i, f: (0, f)),
            pl.BlockSpec((d, tf), lambda b, i, f: (0, f)),
            pl.BlockSpec((tf, d), lambda b, i, f: (f, 0)),
        ],
        out_specs=pl.BlockSpec((None, tm, d), tok),
        out_shape=jax.ShapeDtypeStruct(x.shape, F32),
        scratch_shapes=[pltpu.VMEM((tm, d), BF16), pltpu.VMEM((tm, d), F32)],
        compiler_params=_cparams(("arbitrary", "arbitrary", "arbitrary")),
        name="ffn_ctx" if is_ctx else "ffn",
    )(x, mods_l, g2, w_gate, w_up, w_down)


def _router_kernel(x_ref, mod_ref, g_ref, wr_ref, gates_ref):
    h = _norm_mod(x_ref[...], g_ref[...], mod_ref[4:5, :], mod_ref[3:4, :])
    logits = jnp.dot(h, wr_ref[...], preferred_element_type=F32, precision=lax.Precision.HIGHEST)
    lane = lax.broadcasted_iota(jnp.int32, logits.shape, 1)
    logits = jnp.where(lane < N_EXPERTS, logits, -jnp.inf)
    m1 = jnp.max(logits, axis=-1, keepdims=True)
    i1 = jnp.min(jnp.where(logits == m1, lane, GATE_LANES), axis=-1, keepdims=True)
    rest = jnp.where(lane == i1, -jnp.inf, logits)
    m2 = jnp.max(rest, axis=-1, keepdims=True)
    i2 = jnp.min(jnp.where(rest == m2, lane, GATE_LANES), axis=-1, keepdims=True)
    e2 = jnp.exp(m2 - m1)
    den = 1.0 + e2
    gates_ref[...] = jnp.where(lane == i1, 1.0 / den, 0.0) + jnp.where(lane == i2, e2 / den, 0.0)


def _router(x, mods_l, g2, w_router_pad, *, tm):
    bsz, n_tok, d = x.shape
    tok = lambda b, i: (b, i, 0)
    return pl.pallas_call(
        _router_kernel,
        grid=(bsz, n_tok // tm),
        in_specs=[
            pl.BlockSpec((None, tm, d), tok),
            pl.BlockSpec((None, 6, d), lambda b, i: (b, 0, 0)),
            pl.BlockSpec((1, d), lambda b, i: (0, 0)),
            pl.BlockSpec((d, GATE_LANES), lambda b, i: (0, 0)),
        ],
        out_specs=pl.BlockSpec((None, tm, GATE_LANES), tok),
        out_shape=jax.ShapeDtypeStruct((bsz, n_tok, GATE_LANES), F32),
        compiler_params=_cparams(("arbitrary", "arbitrary")),
        name="router",
    )(x, mods_l, g2, w_router_pad)


def _moe_kernel(x_ref, gates_ref, mod_ref, g_ref, wg_ref, wu_ref, wd_ref, o_ref, h_ref, acc_ref):
    e = pl.program_id(2)
    f = pl.program_id(3)

    @pl.when((e == 0) & (f == 0))
    def _():
        h = _norm_mod(x_ref[...], g_ref[...], mod_ref[4:5, :], mod_ref[3:4, :])
        h_ref[...] = h.astype(BF16)
        acc_ref[...] = jnp.zeros_like(acc_ref)

    gates = gates_ref[...]
    lane = lax.broadcasted_iota(jnp.int32, gates.shape, 1)
    gate = jnp.sum(jnp.where(lane == e, gates, 0.0), axis=-1, keepdims=True)
    h = h_ref[...]
    act = _silu(_dot(h, wg_ref[...])) * _dot(h, wu_ref[...])
    acc_ref[...] += gate * _dot(act.astype(BF16), wd_ref[...])

    @pl.when((e == pl.num_programs(2) - 1) & (f == pl.num_programs(3) - 1))
    def _():
        o_ref[...] = x_ref[...] + mod_ref[5:6, :] * acc_ref[...]


def _moe(x, gates, mods_l, g2, w_gate, w_up, w_down, *, tm, tf):
    bsz, n_tok, d = x.shape
    n_exp, _, d_ff = w_gate.shape
    tok = lambda b, i, e, f: (b, i, 0)
    return pl.pallas_call(
        _moe_kernel,
        grid=(bsz, n_tok // tm, n_exp, d_ff // tf),
        in_specs=[
            pl.BlockSpec((None, tm, d), tok),
            pl.BlockSpec((None, tm, GATE_LANES), tok),
            pl.BlockSpec((None, 6, d), lambda b, i, e, f: (b, 0, 0)),
            pl.BlockSpec((1, d), lambda b, i, e, f: (0, 0)),
            pl.BlockSpec((None, d, tf), lambda b, i, e, f: (e, 0, f)),
            pl.BlockSpec((None, d, tf), lambda b, i, e, f: (e, 0, f)),
            pl.BlockSpec((None, tf, d), lambda b, i, e, f: (e, f, 0)),
        ],
        out_specs=pl.BlockSpec((None, tm, d), tok),
        out_shape=jax.ShapeDtypeStruct(x.shape, F32),
        scratch_shapes=[pltpu.VMEM((tm, d), BF16), pltpu.VMEM((tm, d), F32)],
        compiler_params=_cparams(("arbitrary",) * 4),
        name="moe",
    )(x, gates, mods_l, g2, w_gate, w_up, w_down)


def _attention_bias(rpb, rows):
    n_blocks = rows // ROW_BLOCK
    col = np.arange(GRID_W)
    col_start = np.clip(col - NA_COLS // 2, 0, GRID_W - NA_COLS)
    col_in = (col[None, :] >= col_start[:, None]) & (col[None, :] < col_start[:, None] + NA_COLS)
    col_off = np.clip(col[None, :] - col[:, None], -(NA_COLS - 1), NA_COLS - 1) + (NA_COLS - 1)
    tables = []
    for blk in (0, 1, n_blocks - 1):
        r = blk * ROW_BLOCK + np.arange(ROW_BLOCK)
        win = int(np.clip(blk * ROW_BLOCK - NA_ROWS // 2, 0, rows - KEY_ROWS))
        kr = win + np.arange(KEY_ROWS)
        first = np.clip(r - NA_ROWS // 2, 0, rows - NA_ROWS)
        row_in = (kr[None, :] >= first[:, None]) & (kr[None, :] < first[:, None] + NA_ROWS)
        row_off = np.clip(kr[None, :] - r[:, None] + (NA_ROWS - 1), 0, 2 * NA_ROWS - 2)
        vals = rpb[:, row_off[:, None, :, None], col_off[None, :, None, :]]
        ok = row_in[:, None, :, None] & col_in[None, :, None, :]
        vals = jnp.where(jnp.asarray(ok)[None], vals.astype(F32), NEG_INF)
        tables.append(vals.reshape(C_HEADS, ROW_BLOCK * GRID_W, KEY_ROWS * GRID_W))
    return jnp.stack(tables)


def _block_diag(blocks):
    g, m, n = blocks.shape
    out = jnp.zeros((g * m, g * n), blocks.dtype)
    for i in range(g):
        out = out.at[i * m:(i + 1) * m, i * n:(i + 1) * n].set(blocks[i])
    return out


def kernel(x, c, ctx, c_ctx, w_mod, b_mod, norm1_g, norm2_g, w_in, w_out, a_vn_g, a_vn_b, a_ws, a_bs,
           b_wpool, b_scale, c_qn_g, c_kn_g, c_rpb, ffn_w_gate, ffn_w_up, ffn_w_down,
           moe_w_router, moe_w_gate, moe_w_up, moe_w_down):
    bsz, seq_len, d = x.shape
    depth = w_mod.shape[0]
    rows = seq_len // GRID_W

    cond = jnp.zeros((MOD_ROWS, d), F32).at[:bsz].set(c).at[CTX_MOD_ROW].set(c_ctx)
    mods = _modulation(cond, w_mod, b_mod).reshape(depth, MOD_ROWS, 6, d)
    hsum = _block_diag(jnp.full((C_HEADS, HEAD_DIM, HEAD_DIM), 1.0 / HEAD_DIM, BF16))

    xc = ctx
    for l in range(depth):
        last = l == depth - 1
        fi = l // 2
        mods_l = mods[l]
        g1 = norm1_g[l].reshape(1, d)
        g2 = norm2_g[l].reshape(1, d)
        w_in_l = w_in[l].astype(BF16)
        w_out_l = w_out[l].astype(BF16)
        qg = (jnp.tile(c_qn_g[l], C_HEADS) * ATTN_SCALE).reshape(1, C_WIDTH)
        kg = jnp.tile(c_kn_g[l], C_HEADS).reshape(1, C_WIDTH)
        vn_g = a_vn_g[l].reshape(1, A_WIDTH)
        vn_b = a_vn_b[l].reshape(1, A_WIDTH)
        ws = a_ws[l].astype(BF16)
        bs_full = jnp.repeat(a_bs[l].T, HEAD_DIM, axis=1)
        wpool_bd = _block_diag(b_wpool[l]).astype(BF16)
        bscale = b_scale[l].reshape(1, B_WIDTH)
        bias = _attention_bias(c_rpb[l], rows)
        mix_w = (vn_g, vn_b, ws, bs_full, wpool_bd, bscale, w_out_l)

        uv_c, z_c, q_c, k_c, v_c = _in_proj(xc, mods_l, g1, w_in_l, hsum, qg, kg,
                                            is_ctx=True, tm=xc.shape[1])
        uv, z, q, k, v = _in_proj(x, mods_l, g1, w_in_l, hsum, qg, kg, is_ctx=False, tm=512)
        x = _mix(x, uv, z, q, k, v, k_c, v_c, bias, mods_l, *mix_w)

        if l % 2 == 0:
            ffn_w = (ffn_w_gate[fi].astype(BF16), ffn_w_up[fi].astype(BF16),
                     ffn_w_down[fi].astype(BF16))
            x = _ffn(x, mods_l, g2, *ffn_w, is_ctx=False, tm=1024, tf=256)
        else:
            wr = jnp.zeros((d, GATE_LANES), F32).at[:, :N_EXPERTS].set(moe_w_router[fi])
            gates = _router(x, mods_l, g2, wr, tm=512)
            x = _moe(x, gates, mods_l, g2, moe_w_gate[fi].astype(BF16), moe_w_up[fi].astype(BF16),
                     moe_w_down[fi].astype(BF16), tm=1024, tf=512)

        if not last:
            xc = _mix_ctx(xc, uv_c, z_c, q_c, k_c, v_c, mods_l, *mix_w)
            n_ctx = xc.shape[1]
            if l % 2 == 0:
                xc = _ffn(xc.reshape(1, bsz * n_ctx, d), mods_l, g2, *ffn_w,
                          is_ctx=True, tm=bsz * n_ctx, tf=256).reshape(bsz, n_ctx, d)
            else:
                raise NotImplementedError("context MoE layer is not needed for depth 2")
    return x
```

```python
import functools

import numpy as np
import jax
import jax.numpy as jnp
from jax import lax
from jax.experimental import pallas as pl
from jax.experimental.pallas import tpu as pltpu

F32 = jnp.float32
BF16 = jnp.bfloat16
I32 = jnp.int32

D_MODEL = 1024
GRID_W = 64
HEAD_DIM = 64
A_WIDTH = 256
A_GROUPS = 4
CHUNK = 128
B_WIDTH = 256
POOL_WINDOWS = (2, 4, 8, 16)
POOL_HALO = 8
C_WIDTH = 512
C_HEADS = 8
NA_ROWS = 8
NA_COLS = 16
ATTN_SCALE = HEAD_DIM ** -0.5
IN_COLS = 2 * A_WIDTH + B_WIDTH + 3 * C_WIDTH
N_EXPERTS = 8
TOP_K = 2
EPS = 1e-6
NEG_INF = -1e30

ROW_BLOCK = 4
KEY_ROWS = ROW_BLOCK + NA_ROWS
assert ROW_BLOCK >= NA_ROWS // 2
MOD_ROWS = 8
CTX_MOD_ROW = 4
META_LANES = 128
META_E, META_G, META_RANK = 0, 2, 4
MOE_TM = 512
DMA_UNROLL = 8

VMEM_LIMIT = 60 * 1024 * 1024


def _cparams(sem, **kw):
    return pltpu.CompilerParams(dimension_semantics=sem, vmem_limit_bytes=VMEM_LIMIT, **kw)


def _dot(a, b):
    return jnp.dot(a, b, preferred_element_type=F32)


def _dot_nt(a, b):
    return lax.dot_general(a, b, (((1,), (1,)), ((), ())), preferred_element_type=F32)


def _silu(t):
    return t / (1.0 + jnp.exp(-t))


def _gelu(t):
    return 0.5 * t * (1.0 + lax.erf(t * np.float32(np.sqrt(0.5))))


def _norm_mod(x, g, scale, shift):
    y = x * lax.rsqrt(jnp.mean(x * x, axis=-1, keepdims=True) + EPS) * g
    return y * (1.0 + scale) + shift


def _mod_kernel(cond_ref, w_ref, b_ref, o_ref):
    s = _silu(cond_ref[...])
    o_ref[...] = _dot(s.astype(BF16), w_ref[...].astype(BF16)) + b_ref[...]


def _modulation(cond, w_mod, b_mod):
    depth, d, cols = w_mod.shape
    tn = 1536
    return pl.pallas_call(
        _mod_kernel,
        grid=(depth, cols // tn),
        in_specs=[
            pl.BlockSpec((MOD_ROWS, d), lambda l, j: (0, 0)),
            pl.BlockSpec((None, d, tn), lambda l, j: (l, 0, j)),
            pl.BlockSpec((None, 1, tn), lambda l, j: (l, 0, j)),
        ],
        out_specs=pl.BlockSpec((None, MOD_ROWS, tn), lambda l, j: (l, 0, j)),
        out_shape=jax.ShapeDtypeStruct((depth, MOD_ROWS, cols), F32),
        compiler_params=_cparams(("arbitrary", "arbitrary")),
        name="modulation",
    )(cond, w_mod, b_mod.reshape(depth, 1, cols))


def _head_rms(t, hsum):
    tt = t * t
    hi = tt.astype(BF16)
    lo = (tt - hi.astype(F32)).astype(BF16)
    ms = _dot(hi, hsum) + _dot(lo, hsum)
    return t * lax.rsqrt(ms + EPS)


def _in_proj_kernel(x_ref, mod_ref, g_ref, w_ref, hs_ref, qg_ref, kg_ref,
                    uv_ref, z_ref, q_ref, k_ref, v_ref):
    h = _norm_mod(x_ref[...], g_ref[...], mod_ref[1:2, :], mod_ref[0:1, :])
    p = _dot(h.astype(BF16), w_ref[...])
    o = 2 * A_WIDTH
    uv_ref[...] = p[:, :o].astype(BF16)
    z_ref[...] = p[:, o:o + B_WIDTH]
    o += B_WIDTH
    hs = hs_ref[...]
    q_ref[...] = (_head_rms(p[:, o:o + C_WIDTH], hs) * qg_ref[...]).astype(BF16)
    o += C_WIDTH
    k_ref[...] = (_head_rms(p[:, o:o + C_WIDTH], hs) * kg_ref[...]).astype(BF16)
    o += C_WIDTH
    v_ref[...] = p[:, o:o + C_WIDTH].astype(BF16)


def _in_proj(x, mods_l, g1, w_in, hsum, qg, kg, *, is_ctx, tm):
    bsz, n_tok, d = x.shape
    if is_ctx:
        mod_map = lambda b, i: (CTX_MOD_ROW, 0, 0)
    else:
        mod_map = lambda b, i: (b, 0, 0)
    const = lambda b, i: (0, 0)
    tok = lambda b, i: (b, i, 0)

    def out(width, dtype):
        return (pl.BlockSpec((None, tm, width), tok),
                jax.ShapeDtypeStruct((bsz, n_tok, width), dtype))

    outs = [out(2 * A_WIDTH, BF16), out(B_WIDTH, F32), out(C_WIDTH, BF16),
            out(C_WIDTH, BF16), out(C_WIDTH, BF16)]
    return pl.pallas_call(
        _in_proj_kernel,
        grid=(bsz, n_tok // tm),
        in_specs=[
            pl.BlockSpec((None, tm, d), tok),
            pl.BlockSpec((None, 6, d), mod_map),
            pl.BlockSpec((1, d), const),
            pl.BlockSpec((d, IN_COLS), const),
            pl.BlockSpec((C_WIDTH, C_WIDTH), const),
            pl.BlockSpec((1, C_WIDTH), const),
            pl.BlockSpec((1, C_WIDTH), const),
        ],
        out_specs=[o[0] for o in outs],
        out_shape=[o[1] for o in outs],
        compiler_params=_cparams(("arbitrary", "arbitrary")),
        name="in_proj_ctx" if is_ctx else "in_proj",
    )(x, mods_l, g1, w_in, hsum, qg, kg)


def _mixer_a(uv, vn_g, vn_b, ws_ref, bs_full):
    n_tok = uv.shape[0]
    u = _gelu(uv[:, :A_WIDTH])
    v = _gelu(uv[:, A_WIDTH:])
    vc = v - jnp.mean(v, axis=-1, keepdims=True)
    var = jnp.mean(vc * vc, axis=-1, keepdims=True)
    v = (vc * lax.rsqrt(var + EPS) * vn_g + vn_b).astype(BF16)
    lane_group = lax.broadcasted_iota(I32, (CHUNK, A_WIDTH), 1) // HEAD_DIM
    outs = []
    for c in range(n_tok // CHUNK):
        v_c = v[c * CHUNK:(c + 1) * CHUNK]
        mixed = bs_full
        for g in range(A_GROUPS):
            mixed = mixed + jnp.where(lane_group == g, _dot(ws_ref[g], v_c), 0.0)
        outs.append(u[c * CHUNK:(c + 1) * CHUNK] * mixed)
    return jnp.concatenate(outs, axis=0) if len(outs) > 1 else outs[0]


def _mixer_b(zh_ref, t0, n_tok, seq_len, wpool_bd, b_scale):
    def sh(d):
        return zh_ref[POOL_HALO + d:POOL_HALO + d + n_tok, :]

    z = sh(0)
    s = sh(-1) + z
    sums = [s]
    for win in POOL_WINDOWS[1:]:
        half = win // 2
        for d in range(-half, -half // 2):
            s = s + sh(d)
        for d in range(half // 2, half):
            s = s + sh(d)
        sums.append(s)
    lane_group = lax.broadcasted_iota(I32, (n_tok, B_WIDTH), 1) // (B_WIDTH // len(POOL_WINDOWS))
    pos = t0 + lax.broadcasted_iota(I32, (n_tok, B_WIDTH), 0)
    half = jnp.left_shift(1, lane_group)
    cnt = jnp.minimum(pos + half, seq_len) - jnp.maximum(pos - half, 0)
    total = sums[0]
    for g in range(1, len(POOL_WINDOWS)):
        total = jnp.where(lane_group == g, sums[g], total)
    y = total / cnt.astype(F32) - z
    return _dot(y.astype(BF16), wpool_bd) * b_scale


def _fill_halo(zh_ref, z_ref, t0, n_tok, seq_len):
    zh_ref[POOL_HALO:POOL_HALO + n_tok, :] = z_ref[pl.ds(t0, n_tok), :]
    lo = jnp.maximum(t0 - POOL_HALO, 0)
    hi = jnp.minimum(t0 + n_tok, seq_len - POOL_HALO)
    lo = pl.multiple_of(lo, POOL_HALO)
    hi = pl.multiple_of(hi, POOL_HALO)
    before = z_ref[pl.ds(lo, POOL_HALO), :]
    after = z_ref[pl.ds(hi, POOL_HALO), :]
    zh_ref[0:POOL_HALO, :] = jnp.where(t0 > 0, before, 0.0)
    zh_ref[POOL_HALO + n_tok:, :] = jnp.where(t0 + n_tok < seq_len, after, 0.0)


def _attend(q_h, k_h, v_h, bias_h, kc_h, vc_h):
    s = _dot_nt(q_h, k_h)
    if bias_h is not None:
        s = s + bias_h
    m = jnp.max(s, axis=-1, keepdims=True)
    if kc_h is not None:
        sc = _dot_nt(q_h, kc_h)
        m = jnp.maximum(m, jnp.max(sc, axis=-1, keepdims=True))
        pc = jnp.exp(sc - m)
    p = jnp.exp(s - m)
    den = jnp.sum(p, axis=-1, keepdims=True)
    o = _dot(p.astype(BF16), v_h)
    if kc_h is not None:
        den = den + jnp.sum(pc, axis=-1, keepdims=True)
        o = o + _dot(pc.astype(BF16), vc_h)
    return o / den


def _mix_kernel(x_ref, uv_ref, z_ref, q_ref, k_ref, v_ref, kc_ref, vc_ref, bias_ref, mod_ref,
                vn_g_ref, vn_b_ref, ws_ref, bs_ref, wpool_ref, bscale_ref, wout_ref,
                o_ref, zh_ref, mix_ref, *, seq_len):
    j = pl.program_id(1)
    n_tok = ROW_BLOCK * GRID_W
    t0 = pl.multiple_of(j * n_tok, n_tok)
    rows = seq_len // GRID_W

    a = _mixer_a(uv_ref[...].astype(F32), vn_g_ref[...], vn_b_ref[...], ws_ref, bs_ref[...])
    mix_ref[:, :A_WIDTH] = a.astype(BF16)

    _fill_halo(zh_ref, z_ref, t0, n_tok, seq_len)
    bmix = _mixer_b(zh_ref, t0, n_tok, seq_len, wpool_ref[...], bscale_ref[...])
    mix_ref[:, A_WIDTH:A_WIDTH + B_WIDTH] = bmix.astype(BF16)

    win_row = jnp.clip(j * ROW_BLOCK - NA_ROWS // 2, 0, rows - KEY_ROWS)
    koff = pl.multiple_of(win_row * GRID_W, GRID_W)
    c_off = A_WIDTH + B_WIDTH
    for h in range(C_HEADS):
        hs = slice(h * HEAD_DIM, (h + 1) * HEAD_DIM)
        o = _attend(q_ref[:, hs],
                    k_ref[pl.ds(koff, KEY_ROWS * GRID_W), hs],
                    v_ref[pl.ds(koff, KEY_ROWS * GRID_W), hs],
                    bias_ref[h], kc_ref[:, hs], vc_ref[:, hs])
        mix_ref[:, c_off + h * HEAD_DIM:c_off + (h + 1) * HEAD_DIM] = o.astype(BF16)

    y = _dot(mix_ref[...], wout_ref[...])
    o_ref[...] = x_ref[...] + mod_ref[2:3, :] * y


def _mix(x, uv, z, q, k, v, kc, vc, bias, mods_l, vn_g, vn_b, ws, bs_full, wpool_bd, b_scale, w_out):
    bsz, seq_len, d = x.shape
    n_ctx = kc.shape[1]
    n_tok = ROW_BLOCK * GRID_W
    n_blocks = seq_len // n_tok
    tok = lambda b, j: (b, j, 0)
    full = lambda b, j: (b, 0, 0)
    c2 = lambda b, j: (0, 0)
    c3 = lambda b, j: (0, 0, 0)

    def bias_map(b, j):
        kind = jnp.where(j == 0, 0, jnp.where(j == n_blocks - 1, 2, 1))
        return (kind, 0, 0, 0)

    return pl.pallas_call(
        functools.partial(_mix_kernel, seq_len=seq_len),
        grid=(bsz, n_blocks),
        in_specs=[
            pl.BlockSpec((None, n_tok, d), tok),
            pl.BlockSpec((None, n_tok, 2 * A_WIDTH), tok),
            pl.BlockSpec((None, seq_len, B_WIDTH), full),
            pl.BlockSpec((None, n_tok, C_WIDTH), tok),
            pl.BlockSpec((None, seq_len, C_WIDTH), full),
            pl.BlockSpec((None, seq_len, C_WIDTH), full),
            pl.BlockSpec((None, n_ctx, C_WIDTH), full),
            pl.BlockSpec((None, n_ctx, C_WIDTH), full),
            pl.BlockSpec((None, C_HEADS, n_tok, KEY_ROWS * GRID_W), bias_map),
            pl.BlockSpec((None, 6, d), full),
            pl.BlockSpec((1, A_WIDTH), c2),
            pl.BlockSpec((1, A_WIDTH), c2),
            pl.BlockSpec((A_GROUPS, CHUNK, CHUNK), c3),
            pl.BlockSpec((CHUNK, A_WIDTH), c2),
            pl.BlockSpec((B_WIDTH, B_WIDTH), c2),
            pl.BlockSpec((1, B_WIDTH), c2),
            pl.BlockSpec((d, d), c2),
        ],
        out_specs=pl.BlockSpec((None, n_tok, d), tok),
        out_shape=jax.ShapeDtypeStruct(x.shape, F32),
        scratch_shapes=[
            pltpu.VMEM((n_tok + 2 * POOL_HALO, B_WIDTH), F32),
            pltpu.VMEM((n_tok, d), BF16),
        ],
        compiler_params=_cparams(("arbitrary", "arbitrary")),
        name="mix",
    )(x, uv, z, q, k, v, kc, vc, bias, mods_l, vn_g, vn_b, ws, bs_full, wpool_bd, b_scale, w_out)


def _mix_ctx_kernel(x_ref, uv_ref, z_ref, q_ref, k_ref, v_ref, mod_ref,
                    vn_g_ref, vn_b_ref, ws_ref, bs_ref, wpool_ref, bscale_ref, wout_ref,
                    o_ref, zh_ref, mix_ref):
    n_tok = x_ref.shape[0]
    a = _mixer_a(uv_ref[...].astype(F32), vn_g_ref[...], vn_b_ref[...], ws_ref, bs_ref[...])
    mix_ref[:, :A_WIDTH] = a.astype(BF16)

    zero = jnp.zeros((POOL_HALO, B_WIDTH), F32)
    zh_ref[0:POOL_HALO, :] = zero
    zh_ref[POOL_HALO:POOL_HALO + n_tok, :] = z_ref[...]
    zh_ref[POOL_HALO + n_tok:, :] = zero
    bmix = _mixer_b(zh_ref, 0, n_tok, n_tok, wpool_ref[...], bscale_ref[...])
    mix_ref[:, A_WIDTH:A_WIDTH + B_WIDTH] = bmix.astype(BF16)

    c_off = A_WIDTH + B_WIDTH
    for h in range(C_HEADS):
        hs = slice(h * HEAD_DIM, (h + 1) * HEAD_DIM)
        o = _attend(q_ref[:, hs], k_ref[:, hs], v_ref[:, hs], None, None, None)
        mix_ref[:, c_off + h * HEAD_DIM:c_off + (h + 1) * HEAD_DIM] = o.astype(BF16)

    y = _dot(mix_ref[...], wout_ref[...])
    o_ref[...] = x_ref[...] + mod_ref[2:3, :] * y


def _mix_ctx(x, uv, z, q, k, v, mods_l, vn_g, vn_b, ws, bs_full, wpool_bd, b_scale, w_out):
    bsz, n_tok, d = x.shape
    full = lambda b: (b, 0, 0)
    c2 = lambda b: (0, 0)
    c3 = lambda b: (0, 0, 0)
    return pl.pallas_call(
        _mix_ctx_kernel,
        grid=(bsz,),
        in_specs=[
            pl.BlockSpec((None, n_tok, d), full),
            pl.BlockSpec((None, n_tok, 2 * A_WIDTH), full),
            pl.BlockSpec((None, n_tok, B_WIDTH), full),
            pl.BlockSpec((None, n_tok, C_WIDTH), full),
            pl.BlockSpec((None, n_tok, C_WIDTH), full),
            pl.BlockSpec((None, n_tok, C_WIDTH), full),
            pl.BlockSpec((None, 6, d), lambda b: (CTX_MOD_ROW, 0, 0)),
            pl.BlockSpec((1, A_WIDTH), c2),
            pl.BlockSpec((1, A_WIDTH), c2),
            pl.BlockSpec((A_GROUPS, CHUNK, CHUNK), c3),
            pl.BlockSpec((CHUNK, A_WIDTH), c2),
            pl.BlockSpec((B_WIDTH, B_WIDTH), c2),
            pl.BlockSpec((1, B_WIDTH), c2),
            pl.BlockSpec((d, d), c2),
        ],
        out_specs=pl.BlockSpec((None, n_tok, d), full),
        out_shape=jax.ShapeDtypeStruct(x.shape, F32),
        scratch_shapes=[
            pltpu.VMEM((n_tok + 2 * POOL_HALO, B_WIDTH), F32),
            pltpu.VMEM((n_tok, d), BF16),
        ],
        compiler_params=_cparams(("arbitrary",)),
        name="mix_ctx",
    )(x, uv, z, q, k, v, mods_l, vn_g, vn_b, ws, bs_full, wpool_bd, b_scale, w_out)


def _ffn_kernel(x_ref, mod_ref, g_ref, wg_ref, wu_ref, wd_ref, o_ref, h_ref, acc_ref):
    f = pl.program_id(2)

    @pl.when(f == 0)
    def _():
        h = _norm_mod(x_ref[...], g_ref[...], mod_ref[4:5, :], mod_ref[3:4, :])
        h_ref[...] = h.astype(BF16)
        acc_ref[...] = jnp.zeros_like(acc_ref)

    h = h_ref[...]
    act = _silu(_dot(h, wg_ref[...])) * _dot(h, wu_ref[...])
    acc_ref[...] += _dot(act.astype(BF16), wd_ref[...])

    @pl.when(f == pl.num_programs(2) - 1)
    def _():
        o_ref[...] = x_ref[...] + mod_ref[5:6, :] * acc_ref[...]


def _ffn(x, mods_l, g2, w_gate, w_up, w_down, *, is_ctx, tm, tf):
    bsz, n_tok, d = x.shape
    d_ff = w_gate.shape[1]
    if is_ctx:
        mod_map = lambda b, i, f: (CTX_MOD_ROW, 0, 0)
    else:
        mod_map = lambda b, i, f: (b, 0, 0)
    tok = lambda b, i, f: (b, i, 0)
    return pl.pallas_call(
        _ffn_kernel,
        grid=(bsz, n_tok // tm, d_ff // tf),
        in_specs=[
            pl.BlockSpec((None, tm, d), tok),
            pl.BlockSpec((None, 6, d), mod_map),
            pl.BlockSpec((1, d), lambda b, i, f: (0, 0)),
            pl.BlockSpec((d, tf), lambda b, i, f: (0, f)),
            pl.BlockSpec((d, tf), lambda b, i, f: (0, f)),
            pl.BlockSpec((tf, d), lambda b, i, f: (f, 0)),
        ],
        out_specs=pl.BlockSpec((None, tm, d), tok),
        out_shape=jax.ShapeDtypeStruct(x.shape, F32),
        scratch_shapes=[pltpu.VMEM((tm, d), BF16), pltpu.VMEM((tm, d), F32)],
        compiler_params=_cparams(("arbitrary", "arbitrary", "arbitrary")),
        name="ffn_ctx" if is_ctx else "ffn",
    )(x, mods_l, g2, w_gate, w_up, w_down)


def _route_kernel(x_ref, mod_ref, g_ref, wr_ref, h_ref, meta_ref, cnt_ref, run_ref):
    @pl.when((pl.program_id(0) == 0) & (pl.program_id(1) == 0))
    def _():
        run_ref[...] = jnp.zeros_like(run_ref)

    h = _norm_mod(x_ref[...], g_ref[...], mod_ref[4:5, :], mod_ref[3:4, :])
    h_ref[...] = h
    logits = jnp.dot(h, wr_ref[...], preferred_element_type=F32, precision=lax.Precision.HIGHEST)
    tm = logits.shape[0]
    lane = lax.broadcasted_iota(I32, logits.shape, 1)
    logits = jnp.where(lane < N_EXPERTS, logits, -jnp.inf)
    m1 = jnp.max(logits, axis=-1, keepdims=True)
    i1 = jnp.min(jnp.where(logits == m1, lane, META_LANES), axis=-1, keepdims=True)
    rest = jnp.where(lane == i1, -jnp.inf, logits)
    m2 = jnp.max(rest, axis=-1, keepdims=True)
    i2 = jnp.min(jnp.where(rest == m2, lane, META_LANES), axis=-1, keepdims=True)
    e2 = jnp.exp(m2 - m1)
    den = 1.0 + e2

    pick1 = lane == i1
    pick2 = lane == i2
    onehot = jnp.where(pick1, 1.0, 0.0) + jnp.where(pick2, 1.0, 0.0)
    earlier = (lax.broadcasted_iota(I32, (tm, tm), 0) > lax.broadcasted_iota(I32, (tm, tm), 1))
    before = run_ref[...] + _dot(jnp.where(earlier, 1.0, 0.0).astype(BF16), onehot.astype(BF16))
    r1 = jnp.sum(jnp.where(pick1, before, 0.0), axis=-1, keepdims=True)
    r2 = jnp.sum(jnp.where(pick2, before, 0.0), axis=-1, keepdims=True)
    run_ref[...] += jnp.sum(onehot, axis=0, keepdims=True)
    cnt_ref[...] = run_ref[...]

    meta = jnp.zeros(logits.shape, F32)
    for col, val in ((META_E, i1.astype(F32)), (META_E + 1, i2.astype(F32)),
                     (META_G, 1.0 / den), (META_G + 1, e2 / den),
                     (META_RANK, r1), (META_RANK + 1, r2)):
        meta = jnp.where(lane == col, val, meta)
    meta_ref[...] = meta


def _route(x, mods_l, g2, w_router_pad, *, tm):
    bsz, n_tok, d = x.shape
    tok = lambda b, i: (b, i, 0)
    return pl.pallas_call(
        _route_kernel,
        grid=(bsz, n_tok // tm),
        in_specs=[
            pl.BlockSpec((None, tm, d), tok),
            pl.BlockSpec((None, 6, d), lambda b, i: (b, 0, 0)),
            pl.BlockSpec((1, d), lambda b, i: (0, 0)),
            pl.BlockSpec((d, META_LANES), lambda b, i: (0, 0)),
        ],
        out_specs=[
            pl.BlockSpec((None, tm, d), tok),
            pl.BlockSpec((None, tm, META_LANES), tok),
            pl.BlockSpec((1, META_LANES), lambda b, i: (0, 0)),
        ],
        out_shape=[
            jax.ShapeDtypeStruct((bsz, n_tok, d), F32),
            jax.ShapeDtypeStruct((bsz, n_tok, META_LANES), F32),
            jax.ShapeDtypeStruct((1, META_LANES), F32),
        ],
        scratch_shapes=[pltpu.VMEM((1, META_LANES), F32)],
        compiler_params=_cparams(("arbitrary", "arbitrary")),
        name="route",
    )(x, mods_l, g2, w_router_pad)


def _row_copy(src_ref, src_row, dst_ref, dst_row, sem):
    return pltpu.make_async_copy(src_ref.at[pl.ds(src_row, 1), :], dst_ref.at[pl.ds(dst_row, 1), :], sem)


def _dispatch_kernel(pos_ref, pad_start_ref, pad_count_ref, n_used_ref, h_ref, xs_ref,
                     zero_ref, sem, zsem):
    tm = h_ref.shape[0]
    step = pl.program_id(0)

    @pl.when(step == 0)
    def _():
        zero_ref[...] = jnp.zeros_like(zero_ref)

        def tile_copy(t):
            return pltpu.make_async_copy(zero_ref, xs_ref.at[pl.ds(t * tm, tm), :], zsem)

        def fill_tile(t, carry):
            tile_copy(t).start()
            return carry

        def drain_tile(t, carry):
            tile_copy(t).wait()
            return carry

        n_tiles = xs_ref.shape[0] // tm
        lax.fori_loop(n_used_ref[0], n_tiles, fill_tile, 0)
        lax.fori_loop(n_used_ref[0], n_tiles, drain_tile, 0)
        for e in range(N_EXPERTS):
            start = pad_start_ref[e]

            def fill(k, carry):
                _row_copy(zero_ref, 0, xs_ref, start + k, zsem).start()
                return carry

            def drain(k, carry):
                _row_copy(zero_ref, 0, xs_ref, start + k, zsem).wait()
                return carry

            lax.fori_loop(0, pad_count_ref[e], fill, 0)
            lax.fori_loop(0, pad_count_ref[e], drain, 0)

    base = step * tm

    def send(r, carry):
        for s in range(TOP_K):
            _row_copy(h_ref, r, xs_ref, pos_ref[TOP_K * (base + r) + s], sem).start()
        return carry

    def done(r, carry):
        for s in range(TOP_K):
            _row_copy(h_ref, r, xs_ref, pos_ref[TOP_K * (base + r) + s], sem).wait()
        return carry

    lax.fori_loop(0, tm, send, 0, unroll=DMA_UNROLL)
    lax.fori_loop(0, tm, done, 0, unroll=DMA_UNROLL)


def _dispatch(h2, pos, pad_start, pad_count, n_used, n_rows, *, tm):
    n_tok, d = h2.shape
    return pl.pallas_call(
        _dispatch_kernel,
        grid_spec=pltpu.PrefetchScalarGridSpec(
            num_scalar_prefetch=4,
            grid=(n_tok // tm,),
            in_specs=[pl.BlockSpec((tm, d), lambda i, *_: (i, 0))],
            out_specs=pl.BlockSpec(memory_space=pl.ANY),
            scratch_shapes=[pltpu.VMEM((tm, d), F32), pltpu.SemaphoreType.DMA(()),
                            pltpu.SemaphoreType.DMA(())],
        ),
        out_shape=jax.ShapeDtypeStruct((n_rows, d), F32),
        compiler_params=_cparams(("arbitrary",), has_side_effects=True),
        name="dispatch",
    )(pos, pad_start, pad_count, n_used, h2)


def _gmoe_kernel(tile_expert_ref, n_used_ref, xs_ref, wg_ref, wu_ref, wd_ref, ys_ref, h_ref):
    t = pl.program_id(0)
    f = pl.program_id(1)
    active = t < n_used_ref[0]

    @pl.when(active & (f == 0))
    def _():
        h_ref[...] = xs_ref[...].astype(BF16)

    @pl.when(active)
    def _():
        h = h_ref[...]
        act = _silu(_dot(h, wg_ref[...])) * _dot(h, wu_ref[...])
        y = _dot(act.astype(BF16), wd_ref[...])

        @pl.when(f == 0)
        def _():
            ys_ref[...] = y

        @pl.when(f > 0)
        def _():
            ys_ref[...] += y

    @pl.when(jnp.logical_not(active) & (f == 0))
    def _():
        ys_ref[...] = jnp.zeros_like(ys_ref)


def _gmoe(xs, tile_expert, n_used, w_gate, w_up, w_down, *, tm, tf):
    n_rows, d = xs.shape
    n_exp, _, d_ff = w_gate.shape
    nf = d_ff // tf

    def x_map(t, f, te, nu):
        return (jnp.minimum(t, nu[0] - 1), 0)

    def f_idx(t, f, nu):
        return jnp.where(t < nu[0], f, nf - 1)

    return pl.pallas_call(
        _gmoe_kernel,
        grid_spec=pltpu.PrefetchScalarGridSpec(
            num_scalar_prefetch=2,
            grid=(n_rows // tm, nf),
            in_specs=[
                pl.BlockSpec((tm, d), x_map),
                pl.BlockSpec((None, d, tf), lambda t, f, te, nu: (te[t], 0, f_idx(t, f, nu))),
                pl.BlockSpec((None, d, tf), lambda t, f, te, nu: (te[t], 0, f_idx(t, f, nu))),
                pl.BlockSpec((None, tf, d), lambda t, f, te, nu: (te[t], f_idx(t, f, nu), 0)),
            ],
            out_specs=pl.BlockSpec((tm, d), lambda t, f, te, nu: (t, 0)),
            scratch_shapes=[pltpu.VMEM((tm, d), BF16)],
        ),
        out_shape=jax.ShapeDtypeStruct((n_rows, d), F32),
        compiler_params=_cparams(("arbitrary", "arbitrary")),
        name="grouped_moe",
    )(tile_expert, n_used, xs, w_gate, w_up, w_down)


def _combine_kernel(pos_ref, x_ref, meta_ref, mod_ref, ys_ref, o_ref, buf_ref, sem):
    tm = x_ref.shape[0]
    base = (pl.program_id(0) * pl.num_programs(1) + pl.program_id(1)) * tm

    def fetch(r, carry):
        for s in range(TOP_K):
            _row_copy(ys_ref, pos_ref[TOP_K * (base + r) + s], buf_ref.at[s], r, sem).start()
        return carry

    def done(r, carry):
        for s in range(TOP_K):
            _row_copy(ys_ref, pos_ref[TOP_K * (base + r) + s], buf_ref.at[s], r, sem).wait()
        return carry

    lax.fori_loop(0, tm, fetch, 0, unroll=DMA_UNROLL)
    lax.fori_loop(0, tm, done, 0, unroll=DMA_UNROLL)
    meta = meta_ref[...]
    y = meta[:, META_G:META_G + 1] * buf_ref[0] + meta[:, META_G + 1:META_G + 2] * buf_ref[1]
    o_ref[...] = x_ref[...] + mod_ref[5:6, :] * y


def _combine(x, meta, mods_l, ys, pos, *, tm):
    bsz, n_tok, d = x.shape
    tok = lambda b, i, *_: (b, i, 0)
    return pl.pallas_call(
        _combine_kernel,
        grid_spec=pltpu.PrefetchScalarGridSpec(
            num_scalar_prefetch=1,
            grid=(bsz, n_tok // tm),
            in_specs=[
                pl.BlockSpec((None, tm, d), tok),
                pl.BlockSpec((None, tm, META_LANES), tok),
                pl.BlockSpec((None, 6, d), lambda b, i, *_: (b, 0, 0)),
                pl.BlockSpec(memory_space=pl.ANY),
            ],
            out_specs=pl.BlockSpec((None, tm, d), tok),
            scratch_shapes=[pltpu.VMEM((TOP_K, tm, d), F32), pltpu.SemaphoreType.DMA(())],
        ),
        out_shape=jax.ShapeDtypeStruct(x.shape, F32),
        compiler_params=_cparams(("arbitrary", "arbitrary")),
        name="combine",
    )(pos, x, meta, mods_l, ys)


def _moe(x, mods_l, g2, w_router, w_gate, w_up, w_down):
    bsz, n_tok, d = x.shape
    tm = MOE_TM
    n_all = bsz * n_tok
    n_tiles = TOP_K * n_all // tm + N_EXPERTS
    wr = jnp.zeros((d, META_LANES), F32).at[:, :N_EXPERTS].set(w_router)
    h2, meta, counts = _route(x, mods_l, g2, wr, tm=tm)

    meta2 = meta.reshape(n_all, META_LANES)
    expert = meta2[:, META_E:META_E + TOP_K].astype(I32)
    rank = meta2[:, META_RANK:META_RANK + TOP_K].astype(I32)
    cnt = counts[0, :N_EXPERTS].astype(I32)
    tiles = (cnt + tm - 1) // tm
    tile_end = jnp.cumsum(tiles)
    row_start = (tile_end - tiles) * tm
    expert_ids = jnp.arange(N_EXPERTS, dtype=I32)
    pos = (rank + jnp.sum(jnp.where(expert[..., None] == expert_ids, row_start, 0), axis=-1)).reshape(-1)
    n_used = tile_end[-1:]
    tile_ids = jnp.arange(n_tiles, dtype=I32)
    tile_expert = jnp.sum(tile_ids[:, None] >= tile_end[None, :], axis=1).astype(I32)
    last_expert = jnp.max(jnp.where(tiles > 0, expert_ids, 0))
    tile_expert = jnp.minimum(tile_expert, last_expert)
    pad_start = row_start + cnt
    pad_count = tiles * tm - cnt

    xs = _dispatch(h2.reshape(n_all, d), pos, pad_start, pad_count, n_used, n_tiles * tm, tm=tm)
    ys = _gmoe(xs, tile_expert, n_used, w_gate, w_up, w_down, tm=tm, tf=512)
    return _combine(x, meta, mods_l, ys, pos, tm=tm)


def _attention_bias(rpb, rows):
    n_blocks = rows // ROW_BLOCK
    col = np.arange(GRID_W)
    col_start = np.clip(col - NA_COLS // 2, 0, GRID_W - NA_COLS)
    col_in = (col[None, :] >= col_start[:, None]) & (col[None, :] < col_start[:, None] + NA_COLS)
    col_off = np.clip(col[None, :] - col[:, None], -(NA_COLS - 1), NA_COLS - 1) + (NA_COLS - 1)
    toep = jnp.full((C_HEADS, 2 * NA_ROWS - 1, GRID_W, GRID_W), NEG_INF, F32)
    for dc in range(2 * NA_COLS - 1):
        sel = jnp.asarray((col_off == dc) & col_in)
        toep = jnp.where(sel, rpb[:, :, dc][:, :, None, None].astype(F32), toep)
    masked = jnp.full((C_HEADS, GRID_W, GRID_W), NEG_INF, F32)
    tables = []
    for blk in (0, 1, n_blocks - 1):
        win = int(np.clip(blk * ROW_BLOCK - NA_ROWS // 2, 0, rows - KEY_ROWS))
        strips = []
        for i in range(ROW_BLOCK):
            r = blk * ROW_BLOCK + i
            first = int(np.clip(r - NA_ROWS // 2, 0, rows - NA_ROWS))
            cells = []
            for jj in range(KEY_ROWS):
                kr = win + jj
                if first <= kr < first + NA_ROWS:
                    cells.append(toep[:, kr - r + NA_ROWS - 1])
                else:
                    cells.append(masked)
            strips.append(jnp.concatenate(cells, axis=-1))
        tables.append(jnp.concatenate(strips, axis=1))
    return jnp.stack(tables)


def _block_diag(blocks):
    g, m, n = blocks.shape
    out = jnp.zeros((g * m, g * n), blocks.dtype)
    for i in range(g):
        out = out.at[i * m:(i + 1) * m, i * n:(i + 1) * n].set(blocks[i])
    return out


def kernel(x, c, ctx, c_ctx, w_mod, b_mod, norm1_g, norm2_g, w_in, w_out, a_vn_g, a_vn_b, a_ws, a_bs,
           b_wpool, b_scale, c_qn_g, c_kn_g, c_rpb, ffn_w_gate, ffn_w_up, ffn_w_down,
           moe_w_router, moe_w_gate, moe_w_up, moe_w_down):
    bsz, seq_len, d = x.shape
    depth = w_mod.shape[0]
    rows = seq_len // GRID_W

    cond = jnp.zeros((MOD_ROWS, d), F32).at[:bsz].set(c).at[CTX_MOD_ROW].set(c_ctx)
    mods = _modulation(cond, w_mod, b_mod).reshape(depth, MOD_ROWS, 6, d)
    hsum = _block_diag(jnp.full((C_HEADS, HEAD_DIM, HEAD_DIM), 1.0 / HEAD_DIM, BF16))

    xc = ctx
    for l in range(depth):
        last = l == depth - 1
        fi = l // 2
        mods_l = mods[l]
        g1 = norm1_g[l].reshape(1, d)
        g2 = norm2_g[l].reshape(1, d)
        w_in_l = w_in[l].astype(BF16)
        w_out_l = w_out[l].astype(BF16)
        qg = (jnp.tile(c_qn_g[l], C_HEADS) * ATTN_SCALE).reshape(1, C_WIDTH)
        kg = jnp.tile(c_kn_g[l], C_HEADS).reshape(1, C_WIDTH)
        vn_g = a_vn_g[l].reshape(1, A_WIDTH)
        vn_b = a_vn_b[l].reshape(1, A_WIDTH)
        ws = a_ws[l].astype(BF16)
        bs_full = jnp.repeat(a_bs[l].T, HEAD_DIM, axis=1)
        wpool_bd = _block_diag(b_wpool[l]).astype(BF16)
        bscale = b_scale[l].reshape(1, B_WIDTH)
        bias = _attention_bias(c_rpb[l], rows)
        mix_w = (vn_g, vn_b, ws, bs_full, wpool_bd, bscale, w_out_l)

        uv_c, z_c, q_c, k_c, v_c = _in_proj(xc, mods_l, g1, w_in_l, hsum, qg, kg,
                                            is_ctx=True, tm=xc.shape[1])
        uv, z, q, k, v = _in_proj(x, mods_l, g1, w_in_l, hsum, qg, kg, is_ctx=False, tm=512)
        x = _mix(x, uv, z, q, k, v, k_c, v_c, bias, mods_l, *mix_w)

        if l % 2 == 0:
            ffn_w = (ffn_w_gate[fi].astype(BF16), ffn_w_up[fi].astype(BF16),
                     ffn_w_down[fi].astype(BF16))
            x = _ffn(x, mods_l, g2, *ffn_w, is_ctx=False, tm=1024, tf=256)
        else:
            x = _moe(x, mods_l, g2, moe_w_router[fi], moe_w_gate[fi].astype(BF16),
                     moe_w_up[fi].astype(BF16), moe_w_down[fi].astype(BF16))

        if not last:
            xc = _mix_ctx(xc, uv_c, z_c, q_c, k_c, v_c, mods_l, *mix_w)
            n_ctx = xc.shape[1]
            if l % 2 == 0:
                xc = _ffn(xc.reshape(1, bsz * n_ctx, d), mods_l, g2, *ffn_w,
                          is_ctx=True, tm=bsz * n_ctx, tf=256).reshape(bsz, n_ctx, d)
            else:
                raise NotImplementedError("a context-stream MoE layer only occurs for depth > 2")
    return x
```

```python
import functools

import numpy as np
import jax
import jax.numpy as jnp
from jax import lax
from jax.experimental import pallas as pl
from jax.experimental.pallas import tpu as pltpu

F32 = jnp.float32
BF16 = jnp.bfloat16
I32 = jnp.int32

D_MODEL = 1024
GRID_W = 64
HEAD_DIM = 64
A_WIDTH = 256
A_GROUPS = 4
CHUNK = 128
B_WIDTH = 256
POOL_WINDOWS = (2, 4, 8, 16)
POOL_HALO = 8
C_WIDTH = 512
C_HEADS = 8
NA_ROWS = 8
NA_COLS = 16
ATTN_SCALE = HEAD_DIM ** -0.5
IN_COLS = 2 * A_WIDTH + B_WIDTH + 3 * C_WIDTH
N_EXPERTS = 8
TOP_K = 2
EPS = 1e-6
NEG_INF = -1e30

ROW_BLOCK = 4
KEY_ROWS = ROW_BLOCK + NA_ROWS
assert ROW_BLOCK >= NA_ROWS // 2
MOD_ROWS = 8
CTX_MOD_ROW = 4
META_LANES = 128
META_E, META_G, META_RANK = 0, 2, 4
MOE_TM = 1024
ROUTE_TM = 512
DMA_UNROLL = 8

VMEM_LIMIT = 60 * 1024 * 1024


def _cparams(sem, **kw):
    return pltpu.CompilerParams(dimension_semantics=sem, vmem_limit_bytes=VMEM_LIMIT, **kw)


def _dot(a, b):
    return jnp.dot(a, b, preferred_element_type=F32)


def _dot_nt(a, b):
    return lax.dot_general(a, b, (((1,), (1,)), ((), ())), preferred_element_type=F32)


def _silu(t):
    return t / (1.0 + jnp.exp(-t))


def _gelu(t):
    return 0.5 * t * (1.0 + lax.erf(t * np.float32(np.sqrt(0.5))))


def _norm_mod(x, g, scale, shift):
    y = x * lax.rsqrt(jnp.mean(x * x, axis=-1, keepdims=True) + EPS) * g
    return y * (1.0 + scale) + shift


def _mod_chunks(mod_ref, is_ctx):
    row = CTX_MOD_ROW if is_ctx else pl.program_id(0)
    m = mod_ref[pl.ds(row, 1), :]
    d = m.shape[1] // 6
    return [m[:, k * d:(k + 1) * d] for k in range(6)]


def _mod_spec(l, d):
    return pl.BlockSpec((None, MOD_ROWS, 6 * d), lambda *_: (l, 0, 0))


def _layer_spec(l, shape):
    zeros = (0,) * len(shape)
    return pl.BlockSpec((None,) + tuple(shape), lambda *_: (l,) + zeros)


def _mod_kernel(cond_ref, w_ref, b_ref, o_ref):
    s = _silu(cond_ref[...])
    o_ref[...] = _dot(s.astype(BF16), w_ref[...].astype(BF16)) + b_ref[...]


def _modulation(cond, w_mod, b_mod):
    depth, d, cols = w_mod.shape
    tn = 1536
    return pl.pallas_call(
        _mod_kernel,
        grid=(depth, cols // tn),
        in_specs=[
            pl.BlockSpec((MOD_ROWS, d), lambda l, j: (0, 0)),
            pl.BlockSpec((None, d, tn), lambda l, j: (l, 0, j)),
            pl.BlockSpec((None, 1, tn), lambda l, j: (l, 0, j)),
        ],
        out_specs=pl.BlockSpec((None, MOD_ROWS, tn), lambda l, j: (l, 0, j)),
        out_shape=jax.ShapeDtypeStruct((depth, MOD_ROWS, cols), F32),
        compiler_params=_cparams(("arbitrary", "arbitrary")),
        name="modulation",
    )(cond, w_mod, b_mod.reshape(depth, 1, cols))


def _head_rms(t, hsum):
    tt = t * t
    hi = tt.astype(BF16)
    lo = (tt - hi.astype(F32)).astype(BF16)
    ms = _dot(hi, hsum) + _dot(lo, hsum)
    return t * lax.rsqrt(ms + EPS)


def _in_proj_kernel(x_ref, mod_ref, g_ref, w_ref, hs_ref, qg_ref, kg_ref,
                    uv_ref, z_ref, q_ref, k_ref, v_ref, *, is_ctx):
    shift, scale = _mod_chunks(mod_ref, is_ctx)[:2]
    h = _norm_mod(x_ref[...], g_ref[...], scale, shift)
    p = _dot(h.astype(BF16), w_ref[...])
    o = 2 * A_WIDTH
    uv_ref[...] = p[:, :o].astype(BF16)
    z_ref[...] = p[:, o:o + B_WIDTH]
    o += B_WIDTH
    hs = hs_ref[...]
    q_ref[...] = (_head_rms(p[:, o:o + C_WIDTH], hs) * qg_ref[...]).astype(BF16)
    o += C_WIDTH
    k_ref[...] = (_head_rms(p[:, o:o + C_WIDTH], hs) * kg_ref[...]).astype(BF16)
    o += C_WIDTH
    v_ref[...] = p[:, o:o + C_WIDTH].astype(BF16)


def _in_proj(x, mods, g1, w_in, hsum, qg, kg, *, l, is_ctx, tm):
    bsz, n_tok, d = x.shape
    const = lambda b, i: (0, 0)
    tok = lambda b, i: (b, i, 0)

    def out(width, dtype):
        return (pl.BlockSpec((None, tm, width), tok),
                jax.ShapeDtypeStruct((bsz, n_tok, width), dtype))

    outs = [out(2 * A_WIDTH, BF16), out(B_WIDTH, F32), out(C_WIDTH, BF16),
            out(C_WIDTH, BF16), out(C_WIDTH, BF16)]
    return pl.pallas_call(
        functools.partial(_in_proj_kernel, is_ctx=is_ctx),
        grid=(bsz, n_tok // tm),
        in_specs=[
            pl.BlockSpec((None, tm, d), tok),
            _mod_spec(l, d),
            _layer_spec(l, (1, d)),
            _layer_spec(l, (d, IN_COLS)),
            pl.BlockSpec((C_WIDTH, C_WIDTH), const),
            _layer_spec(l, (1, C_WIDTH)),
            _layer_spec(l, (1, C_WIDTH)),
        ],
        out_specs=[o[0] for o in outs],
        out_shape=[o[1] for o in outs],
        compiler_params=_cparams(("arbitrary", "arbitrary")),
        name="in_proj_ctx" if is_ctx else "in_proj",
    )(x, mods, g1, w_in, hsum, qg, kg)


def _mixer_a(uv, vn_g, vn_b, ws_ref, bs_full):
    n_tok = uv.shape[0]
    u = _gelu(uv[:, :A_WIDTH])
    v = _gelu(uv[:, A_WIDTH:])
    vc = v - jnp.mean(v, axis=-1, keepdims=True)
    var = jnp.mean(vc * vc, axis=-1, keepdims=True)
    v = (vc * lax.rsqrt(var + EPS) * vn_g + vn_b).astype(BF16)
    lane_group = lax.broadcasted_iota(I32, (CHUNK, A_WIDTH), 1) // HEAD_DIM
    outs = []
    for c in range(n_tok // CHUNK):
        v_c = v[c * CHUNK:(c + 1) * CHUNK]
        mixed = bs_full
        for g in range(A_GROUPS):
            mixed = mixed + jnp.where(lane_group == g, _dot(ws_ref[g], v_c), 0.0)
        outs.append(u[c * CHUNK:(c + 1) * CHUNK] * mixed)
    return jnp.concatenate(outs, axis=0) if len(outs) > 1 else outs[0]


def _mixer_b(zh_ref, t0, n_tok, seq_len, wpool_bd, b_scale):
    def sh(d):
        return zh_ref[POOL_HALO + d:POOL_HALO + d + n_tok, :]

    z = sh(0)
    s = sh(-1) + z
    sums = [s]
    for win in POOL_WINDOWS[1:]:
        half = win // 2
        for d in range(-half, -half // 2):
            s = s + sh(d)
        for d in range(half // 2, half):
            s = s + sh(d)
        sums.append(s)
    lane_group = lax.broadcasted_iota(I32, (n_tok, B_WIDTH), 1) // (B_WIDTH // len(POOL_WINDOWS))
    pos = t0 + lax.broadcasted_iota(I32, (n_tok, B_WIDTH), 0)
    half = jnp.left_shift(1, lane_group)
    cnt = jnp.minimum(pos + half, seq_len) - jnp.maximum(pos - half, 0)
    total = sums[0]
    for g in range(1, len(POOL_WINDOWS)):
        total = jnp.where(lane_group == g, sums[g], total)
    y = total / cnt.astype(F32) - z
    return _dot(y.astype(BF16), wpool_bd) * b_scale


def _fill_halo(zh_ref, z_ref, t0, n_tok, seq_len):
    zh_ref[POOL_HALO:POOL_HALO + n_tok, :] = z_ref[pl.ds(t0, n_tok), :]
    lo = jnp.maximum(t0 - POOL_HALO, 0)
    hi = jnp.minimum(t0 + n_tok, seq_len - POOL_HALO)
    lo = pl.multiple_of(lo, POOL_HALO)
    hi = pl.multiple_of(hi, POOL_HALO)
    before = z_ref[pl.ds(lo, POOL_HALO), :]
    after = z_ref[pl.ds(hi, POOL_HALO), :]
    zh_ref[0:POOL_HALO, :] = jnp.where(t0 > 0, before, 0.0)
    zh_ref[POOL_HALO + n_tok:, :] = jnp.where(t0 + n_tok < seq_len, after, 0.0)


def _attend(q_h, k_h, v_h, bias_h, kc_h, vc_h):
    s = _dot_nt(q_h, k_h)
    if bias_h is not None:
        s = s + bias_h
    m = jnp.max(s, axis=-1, keepdims=True)
    if kc_h is not None:
        sc = _dot_nt(q_h, kc_h)
        m = jnp.maximum(m, jnp.max(sc, axis=-1, keepdims=True))
        pc = jnp.exp(sc - m)
    p = jnp.exp(s - m)
    den = jnp.sum(p, axis=-1, keepdims=True)
    o = _dot(p.astype(BF16), v_h)
    if kc_h is not None:
        den = den + jnp.sum(pc, axis=-1, keepdims=True)
        o = o + _dot(pc.astype(BF16), vc_h)
    return o / den


def _mix_kernel(x_ref, uv_ref, z_ref, q_ref, k_ref, v_ref, kc_ref, vc_ref, bias_ref, mod_ref,
                vn_g_ref, vn_b_ref, ws_ref, bs_ref, wpool_ref, bscale_ref, wout_ref,
                o_ref, zh_ref, mix_ref, *, seq_len):
    gate1 = _mod_chunks(mod_ref, False)[2]
    j = pl.program_id(1)
    n_tok = ROW_BLOCK * GRID_W
    t0 = pl.multiple_of(j * n_tok, n_tok)
    rows = seq_len // GRID_W

    a = _mixer_a(uv_ref[...].astype(F32), vn_g_ref[...], vn_b_ref[...], ws_ref, bs_ref[...])
    mix_ref[:, :A_WIDTH] = a.astype(BF16)

    _fill_halo(zh_ref, z_ref, t0, n_tok, seq_len)
    bmix = _mixer_b(zh_ref, t0, n_tok, seq_len, wpool_ref[...], bscale_ref[...])
    mix_ref[:, A_WIDTH:A_WIDTH + B_WIDTH] = bmix.astype(BF16)

    win_row = jnp.clip(j * ROW_BLOCK - NA_ROWS // 2, 0, rows - KEY_ROWS)
    koff = pl.multiple_of(win_row * GRID_W, GRID_W)
    c_off = A_WIDTH + B_WIDTH
    for h in range(C_HEADS):
        hs = slice(h * HEAD_DIM, (h + 1) * HEAD_DIM)
        o = _attend(q_ref[:, hs],
                    k_ref[pl.ds(koff, KEY_ROWS * GRID_W), hs],
                    v_ref[pl.ds(koff, KEY_ROWS * GRID_W), hs],
                    bias_ref[h], kc_ref[:, hs], vc_ref[:, hs])
        mix_ref[:, c_off + h * HEAD_DIM:c_off + (h + 1) * HEAD_DIM] = o.astype(BF16)

    y = _dot(mix_ref[...], wout_ref[...])
    o_ref[...] = x_ref[...] + gate1 * y


def _mix(x, uv, z, q, k, v, kc, vc, bias, mods, vn_g, vn_b, ws, bs_full, wpool_bd, b_scale, w_out, *, l):
    bsz, seq_len, d = x.shape
    n_ctx = kc.shape[1]
    n_tok = ROW_BLOCK * GRID_W
    n_blocks = seq_len // n_tok
    tok = lambda b, j: (b, j, 0)
    full = lambda b, j: (b, 0, 0)
    c2 = lambda b, j: (0, 0)
    c3 = lambda b, j: (0, 0, 0)

    def bias_map(b, j):
        kind = jnp.where(j == 0, 0, jnp.where(j == n_blocks - 1, 2, 1))
        return (l, kind, 0, 0, 0)

    return pl.pallas_call(
        functools.partial(_mix_kernel, seq_len=seq_len),
        grid=(bsz, n_blocks),
        in_specs=[
            pl.BlockSpec((None, n_tok, d), tok),
            pl.BlockSpec((None, n_tok, 2 * A_WIDTH), tok),
            pl.BlockSpec((None, seq_len, B_WIDTH), full),
            pl.BlockSpec((None, n_tok, C_WIDTH), tok),
            pl.BlockSpec((None, seq_len, C_WIDTH), full),
            pl.BlockSpec((None, seq_len, C_WIDTH), full),
            pl.BlockSpec((None, n_ctx, C_WIDTH), full),
            pl.BlockSpec((None, n_ctx, C_WIDTH), full),
            pl.BlockSpec((None, None, C_HEADS, n_tok, KEY_ROWS * GRID_W), bias_map),
            _mod_spec(l, d),
            _layer_spec(l, (1, A_WIDTH)),
            _layer_spec(l, (1, A_WIDTH)),
            _layer_spec(l, (A_GROUPS, CHUNK, CHUNK)),
            _layer_spec(l, (CHUNK, A_WIDTH)),
            _layer_spec(l, (B_WIDTH, B_WIDTH)),
            _layer_spec(l, (1, B_WIDTH)),
            _layer_spec(l, (d, d)),
        ],
        out_specs=pl.BlockSpec((None, n_tok, d), tok),
        out_shape=jax.ShapeDtypeStruct(x.shape, F32),
        scratch_shapes=[
            pltpu.VMEM((n_tok + 2 * POOL_HALO, B_WIDTH), F32),
            pltpu.VMEM((n_tok, d), BF16),
        ],
        compiler_params=_cparams(("arbitrary", "arbitrary")),
        name="mix",
    )(x, uv, z, q, k, v, kc, vc, bias, mods, vn_g, vn_b, ws, bs_full, wpool_bd, b_scale, w_out)


def _mix_ctx_kernel(x_ref, uv_ref, z_ref, q_ref, k_ref, v_ref, mod_ref,
                    vn_g_ref, vn_b_ref, ws_ref, bs_ref, wpool_ref, bscale_ref, wout_ref,
                    o_ref, zh_ref, mix_ref):
    gate1 = _mod_chunks(mod_ref, True)[2]
    n_tok = x_ref.shape[0]
    a = _mixer_a(uv_ref[...].astype(F32), vn_g_ref[...], vn_b_ref[...], ws_ref, bs_ref[...])
    mix_ref[:, :A_WIDTH] = a.astype(BF16)

    zero = jnp.zeros((POOL_HALO, B_WIDTH), F32)
    zh_ref[0:POOL_HALO, :] = zero
    zh_ref[POOL_HALO:POOL_HALO + n_tok, :] = z_ref[...]
    zh_ref[POOL_HALO + n_tok:, :] = zero
    bmix = _mixer_b(zh_ref, 0, n_tok, n_tok, wpool_ref[...], bscale_ref[...])
    mix_ref[:, A_WIDTH:A_WIDTH + B_WIDTH] = bmix.astype(BF16)

    c_off = A_WIDTH + B_WIDTH
    for h in range(C_HEADS):
        hs = slice(h * HEAD_DIM, (h + 1) * HEAD_DIM)
        o = _attend(q_ref[:, hs], k_ref[:, hs], v_ref[:, hs], None, None, None)
        mix_ref[:, c_off + h * HEAD_DIM:c_off + (h + 1) * HEAD_DIM] = o.astype(BF16)

    y = _dot(mix_ref[...], wout_ref[...])
    o_ref[...] = x_ref[...] + gate1 * y


def _mix_ctx(x, uv, z, q, k, v, mods, vn_g, vn_b, ws, bs_full, wpool_bd, b_scale, w_out, *, l):
    bsz, n_tok, d = x.shape
    full = lambda b: (b, 0, 0)
    c2 = lambda b: (0, 0)
    c3 = lambda b: (0, 0, 0)
    return pl.pallas_call(
        _mix_ctx_kernel,
        grid=(bsz,),
        in_specs=[
            pl.BlockSpec((None, n_tok, d), full),
            pl.BlockSpec((None, n_tok, 2 * A_WIDTH), full),
            pl.BlockSpec((None, n_tok, B_WIDTH), full),
            pl.BlockSpec((None, n_tok, C_WIDTH), full),
            pl.BlockSpec((None, n_tok, C_WIDTH), full),
            pl.BlockSpec((None, n_tok, C_WIDTH), full),
            _mod_spec(l, d),
            _layer_spec(l, (1, A_WIDTH)),
            _layer_spec(l, (1, A_WIDTH)),
            _layer_spec(l, (A_GROUPS, CHUNK, CHUNK)),
            _layer_spec(l, (CHUNK, A_WIDTH)),
            _layer_spec(l, (B_WIDTH, B_WIDTH)),
            _layer_spec(l, (1, B_WIDTH)),
            _layer_spec(l, (d, d)),
        ],
        out_specs=pl.BlockSpec((None, n_tok, d), full),
        out_shape=jax.ShapeDtypeStruct(x.shape, F32),
        scratch_shapes=[
            pltpu.VMEM((n_tok + 2 * POOL_HALO, B_WIDTH), F32),
            pltpu.VMEM((n_tok, d), BF16),
        ],
        compiler_params=_cparams(("arbitrary",)),
        name="mix_ctx",
    )(x, uv, z, q, k, v, mods, vn_g, vn_b, ws, bs_full, wpool_bd, b_scale, w_out)


def _ffn_kernel(x_ref, mod_ref, g_ref, wg_ref, wu_ref, wd_ref, o_ref, h_ref, acc_ref, *, is_ctx):
    f = pl.program_id(2)
    _, _, _, shift, scale, gate = _mod_chunks(mod_ref, is_ctx)

    @pl.when(f == 0)
    def _():
        h = _norm_mod(x_ref[...], g_ref[...], scale, shift)
        h_ref[...] = h.astype(BF16)
        acc_ref[...] = jnp.zeros_like(acc_ref)

    h = h_ref[...]
    act = _silu(_dot(h, wg_ref[...].astype(BF16))) * _dot(h, wu_ref[...].astype(BF16))
    acc_ref[...] += _dot(act.astype(BF16), wd_ref[...].astype(BF16))

    @pl.when(f == pl.num_programs(2) - 1)
    def _():
        o_ref[...] = x_ref[...] + gate * acc_ref[...]


def _ffn(x, mods, g2, w_gate, w_up, w_down, *, l, fi, is_ctx, tm, tf):
    bsz, n_tok, d = x.shape
    d_ff = w_gate.shape[-1]
    tok = lambda b, i, f: (b, i, 0)
    return pl.pallas_call(
        functools.partial(_ffn_kernel, is_ctx=is_ctx),
        grid=(bsz, n_tok // tm, d_ff // tf),
        in_specs=[
            pl.BlockSpec((None, tm, d), tok),
            _mod_spec(l, d),
            _layer_spec(l, (1, d)),
            pl.BlockSpec((None, d, tf), lambda b, i, f: (fi, 0, f)),
            pl.BlockSpec((None, d, tf), lambda b, i, f: (fi, 0, f)),
            pl.BlockSpec((None, tf, d), lambda b, i, f: (fi, f, 0)),
        ],
        out_specs=pl.BlockSpec((None, tm, d), tok),
        out_shape=jax.ShapeDtypeStruct(x.shape, F32),
        scratch_shapes=[pltpu.VMEM((tm, d), BF16), pltpu.VMEM((tm, d), F32)],
        compiler_params=_cparams(("arbitrary", "arbitrary", "arbitrary")),
        name="ffn_ctx" if is_ctx else "ffn",
    )(x, mods, g2, w_gate, w_up, w_down)


def _route_kernel(x_ref, mod_ref, g_ref, wr_ref, h_ref, meta_ref, cnt_ref, run_ref):
    @pl.when((pl.program_id(0) == 0) & (pl.program_id(1) == 0))
    def _():
        run_ref[...] = jnp.zeros_like(run_ref)

    _, _, _, shift, scale, _ = _mod_chunks(mod_ref, False)
    h = _norm_mod(x_ref[...], g_ref[...], scale, shift)
    h_ref[...] = h
    logits = jnp.dot(h, wr_ref[...], preferred_element_type=F32, precision=lax.Precision.HIGHEST)
    tm = logits.shape[0]
    lane = lax.broadcasted_iota(I32, logits.shape, 1)
    logits = jnp.where(lane < N_EXPERTS, logits, -jnp.inf)
    m1 = jnp.max(logits, axis=-1, keepdims=True)
    i1 = jnp.min(jnp.where(logits == m1, lane, META_LANES), axis=-1, keepdims=True)
    rest = jnp.where(lane == i1, -jnp.inf, logits)
    m2 = jnp.max(rest, axis=-1, keepdims=True)
    i2 = jnp.min(jnp.where(rest == m2, lane, META_LANES), axis=-1, keepdims=True)
    e2 = jnp.exp(m2 - m1)
    den = 1.0 + e2

    pick1 = lane == i1
    pick2 = lane == i2
    onehot = jnp.where(pick1, 1.0, 0.0) + jnp.where(pick2, 1.0, 0.0)
    earlier = (lax.broadcasted_iota(I32, (tm, tm), 0) > lax.broadcasted_iota(I32, (tm, tm), 1))
    before = run_ref[...] + _dot(jnp.where(earlier, 1.0, 0.0).astype(BF16), onehot.astype(BF16))
    r1 = jnp.sum(jnp.where(pick1, before, 0.0), axis=-1, keepdims=True)
    r2 = jnp.sum(jnp.where(pick2, before, 0.0), axis=-1, keepdims=True)
    run_ref[...] += jnp.sum(onehot, axis=0, keepdims=True)
    cnt_ref[...] = run_ref[...]

    meta = jnp.zeros(logits.shape, F32)
    for col, val in ((META_E, i1.astype(F32)), (META_E + 1, i2.astype(F32)),
                     (META_G, 1.0 / den), (META_G + 1, e2 / den),
                     (META_RANK, r1), (META_RANK + 1, r2)):
        meta = jnp.where(lane == col, val, meta)
    meta_ref[...] = meta


def _route(x, mods, g2, w_router_pad, *, l, fi, tm):
    bsz, n_tok, d = x.shape
    tok = lambda b, i: (b, i, 0)
    return pl.pallas_call(
        _route_kernel,
        grid=(bsz, n_tok // tm),
        in_specs=[
            pl.BlockSpec((None, tm, d), tok),
            _mod_spec(l, d),
            _layer_spec(l, (1, d)),
            _layer_spec(fi, (d, META_LANES)),
        ],
        out_specs=[
            pl.BlockSpec((None, tm, d), tok),
            pl.BlockSpec((None, tm, META_LANES), tok),
            pl.BlockSpec((1, META_LANES), lambda b, i: (0, 0)),
        ],
        out_shape=[
            jax.ShapeDtypeStruct((bsz, n_tok, d), F32),
            jax.ShapeDtypeStruct((bsz, n_tok, META_LANES), F32),
            jax.ShapeDtypeStruct((1, META_LANES), F32),
        ],
        scratch_shapes=[pltpu.VMEM((1, META_LANES), F32)],
        compiler_params=_cparams(("arbitrary", "arbitrary")),
        name="route",
    )(x, mods, g2, w_router_pad)


def _row_copy(src_ref, src_row, dst_ref, dst_row, sem):
    return pltpu.make_async_copy(src_ref.at[pl.ds(src_row, 1), :], dst_ref.at[pl.ds(dst_row, 1), :], sem)


def _dispatch_kernel(pos_ref, pad_start_ref, pad_count_ref, free_tile_ref, h_ref, xs_ref,
                     zero_ref, sem, zsem):
    tm = h_ref.shape[0]
    step = pl.program_id(0)

    @pl.when(step == 0)
    def _():
        zero_ref[...] = jnp.zeros_like(zero_ref)

        def tile_copy(t):
            return pltpu.make_async_copy(zero_ref, xs_ref.at[pl.ds(t * tm, tm), :], zsem)

        def fill_tile(t, carry):
            tile_copy(t).start()
            return carry

        def drain_tile(t, carry):
            tile_copy(t).wait()
            return carry

        n_tiles = xs_ref.shape[0] // tm
        lax.fori_loop(free_tile_ref[0], n_tiles, fill_tile, 0)
        lax.fori_loop(free_tile_ref[0], n_tiles, drain_tile, 0)
        for e in range(N_EXPERTS):
            start = pad_start_ref[e]

            def fill(k, carry):
                _row_copy(zero_ref, 0, xs_ref, start + k, zsem).start()
                return carry

            def drain(k, carry):
                _row_copy(zero_ref, 0, xs_ref, start + k, zsem).wait()
                return carry

            lax.fori_loop(0, pad_count_ref[e], fill, 0)
            lax.fori_loop(0, pad_count_ref[e], drain, 0)

    base = step * tm

    def send(r, carry):
        for s in range(TOP_K):
            _row_copy(h_ref, r, xs_ref, pos_ref[TOP_K * (base + r) + s], sem).start()
        return carry

    def done(r, carry):
        for s in range(TOP_K):
            _row_copy(h_ref, r, xs_ref, pos_ref[TOP_K * (base + r) + s], sem).wait()
        return carry

    lax.fori_loop(0, tm, send, 0, unroll=DMA_UNROLL)
    lax.fori_loop(0, tm, done, 0, unroll=DMA_UNROLL)


def _dispatch(h2, pos, pad_start, pad_count, free_tile, n_rows, *, tm):
    n_tok, d = h2.shape
    return pl.pallas_call(
        _dispatch_kernel,
        grid_spec=pltpu.PrefetchScalarGridSpec(
            num_scalar_prefetch=4,
            grid=(n_tok // tm,),
            in_specs=[pl.BlockSpec((tm, d), lambda i, *_: (i, 0))],
            out_specs=pl.BlockSpec(memory_space=pl.ANY),
            scratch_shapes=[pltpu.VMEM((tm, d), F32), pltpu.SemaphoreType.DMA(()),
                            pltpu.SemaphoreType.DMA(())],
        ),
        out_shape=jax.ShapeDtypeStruct((n_rows, d), F32),
        compiler_params=_cparams(("arbitrary",), has_side_effects=True),
        name="dispatch",
    )(pos, pad_start, pad_count, free_tile, h2)


def _gmoe_kernel(tile_expert_ref, n_used_ref, xs_ref, wg_ref, wu_ref, wd_ref, ys_ref, h_ref):
    t = pl.program_id(0)
    f = pl.program_id(1)
    active = t < n_used_ref[0]

    @pl.when(active & (f == 0))
    def _():
        h_ref[...] = xs_ref[...].astype(BF16)

    @pl.when(active)
    def _():
        h = h_ref[...]
        act = _silu(_dot(h, wg_ref[...].astype(BF16))) * _dot(h, wu_ref[...].astype(BF16))
        y = _dot(act.astype(BF16), wd_ref[...].astype(BF16))

        @pl.when(f == 0)
        def _():
            ys_ref[...] = y

        @pl.when(f > 0)
        def _():
            ys_ref[...] += y

    @pl.when(jnp.logical_not(active) & (f == 0))
    def _():
        ys_ref[...] = jnp.zeros_like(ys_ref)


def _gmoe(xs, tile_expert, n_used, w_gate, w_up, w_down, *, fi, tm, tf):
    n_rows, d = xs.shape
    d_ff = w_gate.shape[-1]
    nf = d_ff // tf

    def x_map(t, f, te, nu):
        return (jnp.minimum(t, nu[0] - 1), 0)

    def f_idx(t, f, nu):
        return jnp.where(t < nu[0], f, nf - 1)

    return pl.pallas_call(
        _gmoe_kernel,
        grid_spec=pltpu.PrefetchScalarGridSpec(
            num_scalar_prefetch=2,
            grid=(n_rows // tm, nf),
            in_specs=[
                pl.BlockSpec((tm, d), x_map),
                pl.BlockSpec((None, None, d, tf), lambda t, f, te, nu: (fi, te[t], 0, f_idx(t, f, nu))),
                pl.BlockSpec((None, None, d, tf), lambda t, f, te, nu: (fi, te[t], 0, f_idx(t, f, nu))),
                pl.BlockSpec((None, None, tf, d), lambda t, f, te, nu: (fi, te[t], f_idx(t, f, nu), 0)),
            ],
            out_specs=pl.BlockSpec((tm, d), lambda t, f, te, nu: (t, 0)),
            scratch_shapes=[pltpu.VMEM((tm, d), BF16)],
        ),
        out_shape=jax.ShapeDtypeStruct((n_rows, d), F32),
        compiler_params=_cparams(("arbitrary", "arbitrary")),
        name="grouped_moe",
    )(tile_expert, n_used, xs, w_gate, w_up, w_down)


def _combine_kernel(pos_ref, x_ref, meta_ref, mod_ref, ys_ref, o_ref, buf_ref, sem):
    tm = x_ref.shape[0]
    base = (pl.program_id(0) * pl.num_programs(1) + pl.program_id(1)) * tm

    def fetch(r, carry):
        for s in range(TOP_K):
            _row_copy(ys_ref, pos_ref[TOP_K * (base + r) + s], buf_ref.at[s], r, sem).start()
        return carry

    def done(r, carry):
        for s in range(TOP_K):
            _row_copy(ys_ref, pos_ref[TOP_K * (base + r) + s], buf_ref.at[s], r, sem).wait()
        return carry

    lax.fori_loop(0, tm, fetch, 0, unroll=DMA_UNROLL)
    lax.fori_loop(0, tm, done, 0, unroll=DMA_UNROLL)
    meta = meta_ref[...]
    y = meta[:, META_G:META_G + 1] * buf_ref[0] + meta[:, META_G + 1:META_G + 2] * buf_ref[1]
    o_ref[...] = x_ref[...] + _mod_chunks(mod_ref, False)[5] * y


def _combine(x, meta, mods, ys, pos, *, l, tm):
    bsz, n_tok, d = x.shape
    tok = lambda b, i, *_: (b, i, 0)
    return pl.pallas_call(
        _combine_kernel,
        grid_spec=pltpu.PrefetchScalarGridSpec(
            num_scalar_prefetch=1,
            grid=(bsz, n_tok // tm),
            in_specs=[
                pl.BlockSpec((None, tm, d), tok),
                pl.BlockSpec((None, tm, META_LANES), tok),
                _mod_spec(l, d),
                pl.BlockSpec(memory_space=pl.ANY),
            ],
            out_specs=pl.BlockSpec((None, tm, d), tok),
            scratch_shapes=[pltpu.VMEM((TOP_K, tm, d), F32), pltpu.SemaphoreType.DMA(())],
        ),
        out_shape=jax.ShapeDtypeStruct(x.shape, F32),
        compiler_params=_cparams(("arbitrary", "arbitrary")),
        name="combine",
    )(pos, x, meta, mods, ys)


def _moe(x, mods, g2, w_router, w_gate, w_up, w_down, *, l, fi):
    bsz, n_tok, d = x.shape
    tm = MOE_TM
    n_all = bsz * n_tok
    n_tiles = TOP_K * n_all // tm + N_EXPERTS
    wr = jnp.pad(w_router, ((0, 0), (0, 0), (0, META_LANES - N_EXPERTS)))
    h2, meta, counts = _route(x, mods, g2, wr, l=l, fi=fi, tm=ROUTE_TM)

    meta2 = meta.reshape(n_all, META_LANES)
    expert = meta2[:, META_E:META_E + TOP_K].astype(I32)
    rank = meta2[:, META_RANK:META_RANK + TOP_K].astype(I32)
    cnt = counts[0, :N_EXPERTS].astype(I32)
    tiles = (cnt + tm - 1) // tm
    tile_end = jnp.cumsum(tiles)
    row_start = (tile_end - tiles) * tm
    expert_ids = jnp.arange(N_EXPERTS, dtype=I32)
    pos = (rank + jnp.sum(jnp.where(expert[..., None] == expert_ids, row_start, 0), axis=-1)).reshape(-1)
    n_used = tile_end[-1:]
    tile_ids = jnp.arange(n_tiles, dtype=I32)
    tile_expert = jnp.sum(tile_ids[:, None] >= tile_end[None, :], axis=1).astype(I32)
    last_expert = jnp.max(jnp.where(tiles > 0, expert_ids, 0))
    tile_expert = jnp.minimum(tile_expert, last_expert)
    pad_start = row_start + cnt
    pad_count = tiles * tm - cnt

    free_tile = n_used * (tm // ROUTE_TM)
    xs = _dispatch(h2.reshape(n_all, d), pos, pad_start, pad_count, free_tile, n_tiles * tm, tm=ROUTE_TM)
    ys = _gmoe(xs, tile_expert, n_used, w_gate, w_up, w_down, fi=fi, tm=tm, tf=512)
    return _combine(x, meta, mods, ys, pos, l=l, tm=ROUTE_TM)


def _attention_bias(rpb, rows):
    n_blocks = rows // ROW_BLOCK
    col = np.arange(GRID_W)
    col_start = np.clip(col - NA_COLS // 2, 0, GRID_W - NA_COLS)
    col_in = (col[None, :] >= col_start[:, None]) & (col[None, :] < col_start[:, None] + NA_COLS)
    col_off = np.clip(col[None, :] - col[:, None], -(NA_COLS - 1), NA_COLS - 1) + (NA_COLS - 1)
    toep = jnp.full(rpb.shape[:3] + (GRID_W, GRID_W), NEG_INF, F32)
    for dc in range(2 * NA_COLS - 1):
        sel = jnp.asarray((col_off == dc) & col_in)
        toep = jnp.where(sel, rpb[..., dc][..., None, None].astype(F32), toep)
    row_off = np.zeros((3, ROW_BLOCK, KEY_ROWS), np.int32)
    row_in = np.zeros((3, ROW_BLOCK, KEY_ROWS), bool)
    for kind, blk in enumerate((0, 1, n_blocks - 1)):
        win = int(np.clip(blk * ROW_BLOCK - NA_ROWS // 2, 0, rows - KEY_ROWS))
        r = blk * ROW_BLOCK + np.arange(ROW_BLOCK)[:, None]
        kr = win + np.arange(KEY_ROWS)[None, :]
        first = np.clip(r - NA_ROWS // 2, 0, rows - NA_ROWS)
        row_in[kind] = (kr >= first) & (kr < first + NA_ROWS)
        row_off[kind] = np.clip(kr - r + NA_ROWS - 1, 0, 2 * NA_ROWS - 2)
    cells = jnp.take(toep, jnp.asarray(row_off.reshape(-1)), axis=2)
    cells = cells.reshape(rpb.shape[:2] + (3, ROW_BLOCK, KEY_ROWS, GRID_W, GRID_W))
    cells = jnp.where(jnp.asarray(row_in)[:, :, :, None, None], cells, NEG_INF)
    cells = jnp.transpose(cells, (0, 2, 1, 3, 5, 4, 6))
    return cells.reshape(rpb.shape[0], 3, C_HEADS, ROW_BLOCK * GRID_W, KEY_ROWS * GRID_W)


def _block_diag(blocks):
    g, m, n = blocks.shape[-3:]
    eye = jnp.eye(g, dtype=blocks.dtype)
    out = blocks[..., :, :, None, :] * eye[:, None, :, None]
    return out.reshape(blocks.shape[:-3] + (g * m, g * n))


def kernel(x, c, ctx, c_ctx, w_mod, b_mod, norm1_g, norm2_g, w_in, w_out, a_vn_g, a_vn_b, a_ws, a_bs,
           b_wpool, b_scale, c_qn_g, c_kn_g, c_rpb, ffn_w_gate, ffn_w_up, ffn_w_down,
           moe_w_router, moe_w_gate, moe_w_up, moe_w_down):
    bsz, seq_len, d = x.shape
    depth = w_mod.shape[0]
    rows = seq_len // GRID_W
    assert bsz <= CTX_MOD_ROW

    cond = jnp.concatenate([c, jnp.zeros((CTX_MOD_ROW - bsz, d), F32), c_ctx[None, :],
                            jnp.zeros((MOD_ROWS - CTX_MOD_ROW - 1, d), F32)], axis=0)
    mods = _modulation(cond, w_mod, b_mod)
    hsum = jnp.asarray(np.kron(np.eye(C_HEADS), np.full((HEAD_DIM, HEAD_DIM), 1.0 / HEAD_DIM)), BF16)

    g1 = norm1_g.reshape(depth, 1, d)
    g2 = norm2_g.reshape(depth, 1, d)
    w_in_b = w_in.astype(BF16)
    qg = (jnp.tile(c_qn_g, (1, C_HEADS)) * ATTN_SCALE).reshape(depth, 1, C_WIDTH)
    kg = jnp.tile(c_kn_g, (1, C_HEADS)).reshape(depth, 1, C_WIDTH)
    mix_w = (a_vn_g.reshape(depth, 1, A_WIDTH), a_vn_b.reshape(depth, 1, A_WIDTH),
             a_ws.astype(BF16),
             jnp.repeat(jnp.swapaxes(a_bs, 1, 2), HEAD_DIM, axis=2),
             _block_diag(b_wpool).astype(BF16),
             b_scale.reshape(depth, 1, B_WIDTH),
             w_out.astype(BF16))
    bias = _attention_bias(c_rpb, rows)

    xc = ctx
    for l in range(depth):
        last = l == depth - 1
        fi = l // 2
        uv_c, z_c, q_c, k_c, v_c = _in_proj(xc, mods, g1, w_in_b, hsum, qg, kg,
                                            l=l, is_ctx=True, tm=xc.shape[1])
        uv, z, q, k, v = _in_proj(x, mods, g1, w_in_b, hsum, qg, kg, l=l, is_ctx=False, tm=512)
        x = _mix(x, uv, z, q, k, v, k_c, v_c, bias, mods, *mix_w, l=l)

        ffn_w = (ffn_w_gate, ffn_w_up, ffn_w_down)
        if l % 2 == 0:
            x = _ffn(x, mods, g2, *ffn_w, l=l, fi=fi, is_ctx=False, tm=1024, tf=256)
        else:
            x = _moe(x, mods, g2, moe_w_router, moe_w_gate, moe_w_up, moe_w_down, l=l, fi=fi)

        if not last:
            xc = _mix_ctx(xc, uv_c, z_c, q_c, k_c, v_c, mods, *mix_w, l=l)
            n_ctx = xc.shape[1]
            if l % 2 == 0:
                xc = _ffn(xc.reshape(1, bsz * n_ctx, d), mods, g2, *ffn_w, l=l, fi=fi,
                          is_ctx=True, tm=bsz * n_ctx, tf=256).reshape(bsz, n_ctx, d)
            else:
                raise NotImplementedError("a context-stream MoE layer only occurs for depth > 2")
    return x
```

```python
import functools

import numpy as np
import jax
import jax.numpy as jnp
from jax import lax
from jax.experimental import pallas as pl
from jax.experimental.pallas import tpu as pltpu

F32 = jnp.float32
BF16 = jnp.bfloat16
I32 = jnp.int32

D_MODEL = 1024
GRID_W = 64
HEAD_DIM = 64
A_WIDTH = 256
A_GROUPS = 4
CHUNK = 128
B_WIDTH = 256
POOL_WINDOWS = (2, 4, 8, 16)
POOL_HALO = 8
C_WIDTH = 512
C_HEADS = 8
NA_ROWS = 8
NA_COLS = 16
ATTN_SCALE = HEAD_DIM ** -0.5
IN_COLS = 2 * A_WIDTH + B_WIDTH + 3 * C_WIDTH
N_EXPERTS = 8
TOP_K = 2
EPS = 1e-6
NEG_INF = -1e30

ROW_BLOCK = 4
KEY_ROWS = ROW_BLOCK + NA_ROWS
assert ROW_BLOCK >= NA_ROWS // 2
MOD_ROWS = 8
CTX_MOD_ROW = 4
META_LANES = 128
META_E, META_G, META_RANK = 0, 2, 4
MOE_TM = 1024
ROUTE_TM = 512
DMA_UNROLL = 8
SUBLANES = 8

VMEM_LIMIT = 60 * 1024 * 1024


def _cparams(sem, **kw):
    return pltpu.CompilerParams(dimension_semantics=sem, vmem_limit_bytes=VMEM_LIMIT, **kw)


def _dot(a, b):
    return jnp.dot(a, b, preferred_element_type=F32)


def _dot_nt(a, b):
    return lax.dot_general(a, b, (((1,), (1,)), ((), ())), preferred_element_type=F32)


def _dot_tn(a, b):
    return lax.dot_general(a, b, (((0,), (0,)), ((), ())), preferred_element_type=F32)


def _silu(t):
    return t / (1.0 + jnp.exp(-t))


def _gelu(t):
    return 0.5 * t * (1.0 + lax.erf(t * np.float32(np.sqrt(0.5))))


def _norm_mod(x, g, scale, shift):
    y = x * lax.rsqrt(jnp.mean(x * x, axis=-1, keepdims=True) + EPS) * g
    return y * (1.0 + scale) + shift


def _mod_chunks(mod_ref, is_ctx):
    row = CTX_MOD_ROW if is_ctx else pl.program_id(0)
    m = mod_ref[pl.ds(row, 1), :]
    d = m.shape[1] // 6
    return [m[:, k * d:(k + 1) * d] for k in range(6)]


def _mod_spec(l, d):
    return pl.BlockSpec((None, MOD_ROWS, 6 * d), lambda *_: (l, 0, 0))


def _layer_spec(l, shape):
    zeros = (0,) * len(shape)
    return pl.BlockSpec((None,) + tuple(shape), lambda *_: (l,) + zeros)


def _mod_kernel(cond_ref, w_ref, b_ref, o_ref):
    s = _silu(cond_ref[...])
    o_ref[...] = _dot(s.astype(BF16), w_ref[...].astype(BF16)) + b_ref[...]


def _modulation(cond, w_mod, b_mod):
    depth, d, cols = w_mod.shape
    tn = 1536
    return pl.pallas_call(
        _mod_kernel,
        grid=(depth, cols // tn),
        in_specs=[
            pl.BlockSpec((MOD_ROWS, d), lambda l, j: (0, 0)),
            pl.BlockSpec((None, d, tn), lambda l, j: (l, 0, j)),
            pl.BlockSpec((None, 1, tn), lambda l, j: (l, 0, j)),
        ],
        out_specs=pl.BlockSpec((None, MOD_ROWS, tn), lambda l, j: (l, 0, j)),
        out_shape=jax.ShapeDtypeStruct((depth, MOD_ROWS, cols), F32),
        compiler_params=_cparams(("arbitrary", "arbitrary")),
        name="modulation",
    )(cond, w_mod, b_mod.reshape(depth, 1, cols))


def _head_rms(t, hsum):
    ms = _dot((t * t).astype(BF16), hsum)
    return t * lax.rsqrt(ms + EPS)


def _in_proj_kernel(x_ref, mod_ref, g_ref, w_ref, hs_ref, qg_ref, kg_ref,
                    uv_ref, z_ref, q_ref, k_ref, v_ref, *, is_ctx):
    shift, scale = _mod_chunks(mod_ref, is_ctx)[:2]
    h = _norm_mod(x_ref[...], g_ref[...], scale, shift)
    p = _dot(h.astype(BF16), w_ref[...])
    o = 2 * A_WIDTH
    uv_ref[...] = p[:, :o].astype(BF16)
    z_ref[...] = p[:, o:o + B_WIDTH]
    o += B_WIDTH
    hs = hs_ref[...]
    q_ref[...] = (_head_rms(p[:, o:o + C_WIDTH], hs) * qg_ref[...]).astype(BF16)
    o += C_WIDTH
    k_ref[...] = (_head_rms(p[:, o:o + C_WIDTH], hs) * kg_ref[...]).astype(BF16)
    o += C_WIDTH
    v_ref[...] = p[:, o:o + C_WIDTH].astype(BF16)


def _in_proj(x, mods, g1, w_in, hsum, qg, kg, *, l, is_ctx, tm):
    bsz, n_tok, d = x.shape
    const = lambda b, i: (0, 0)
    tok = lambda b, i: (b, i, 0)

    def out(width, dtype):
        return (pl.BlockSpec((None, tm, width), tok),
                jax.ShapeDtypeStruct((bsz, n_tok, width), dtype))

    outs = [out(2 * A_WIDTH, BF16), out(B_WIDTH, F32), out(C_WIDTH, BF16),
            out(C_WIDTH, BF16), out(C_WIDTH, BF16)]
    return pl.pallas_call(
        functools.partial(_in_proj_kernel, is_ctx=is_ctx),
        grid=(bsz, n_tok // tm),
        in_specs=[
            pl.BlockSpec((None, tm, d), tok),
            _mod_spec(l, d),
            _layer_spec(l, (1, d)),
            _layer_spec(l, (d, IN_COLS)),
            pl.BlockSpec((C_WIDTH, C_WIDTH), const),
            _layer_spec(l, (1, C_WIDTH)),
            _layer_spec(l, (1, C_WIDTH)),
        ],
        out_specs=[o[0] for o in outs],
        out_shape=[o[1] for o in outs],
        compiler_params=_cparams(("arbitrary", "arbitrary")),
        name="in_proj_ctx" if is_ctx else "in_proj",
    )(x, mods, g1, w_in, hsum, qg, kg)


def _mixer_a(uv, vn_g, vn_b, ws_ref, bs_full):
    n_tok = uv.shape[0]
    u = _gelu(uv[:, :A_WIDTH])
    v = _gelu(uv[:, A_WIDTH:])
    vc = v - jnp.mean(v, axis=-1, keepdims=True)
    var = jnp.mean(vc * vc, axis=-1, keepdims=True)
    v = (vc * lax.rsqrt(var + EPS) * vn_g + vn_b).astype(BF16)
    lane_group = lax.broadcasted_iota(I32, (CHUNK, A_WIDTH), 1) // HEAD_DIM
    outs = []
    for c in range(n_tok // CHUNK):
        v_c = v[c * CHUNK:(c + 1) * CHUNK]
        mixed = bs_full
        for g in range(A_GROUPS):
            mixed = mixed + jnp.where(lane_group == g, _dot(ws_ref[g], v_c), 0.0)
        outs.append(u[c * CHUNK:(c + 1) * CHUNK] * mixed)
    return jnp.concatenate(outs, axis=0) if len(outs) > 1 else outs[0]


def _mixer_b(zh_ref, t0, n_tok, seq_len, wpool_bd, b_scale):
    def sh(d):
        return zh_ref[POOL_HALO + d:POOL_HALO + d + n_tok, :]

    z = sh(0)
    s = sh(-1) + z
    sums = [s]
    for win in POOL_WINDOWS[1:]:
        half = win // 2
        for d in range(-half, -half // 2):
            s = s + sh(d)
        for d in range(half // 2, half):
            s = s + sh(d)
        sums.append(s)
    lane_group = lax.broadcasted_iota(I32, (n_tok, B_WIDTH), 1) // (B_WIDTH // len(POOL_WINDOWS))
    pos = t0 + lax.broadcasted_iota(I32, (n_tok, B_WIDTH), 0)
    half = jnp.left_shift(1, lane_group)
    cnt = jnp.minimum(pos + half, seq_len) - jnp.maximum(pos - half, 0)
    total = sums[0]
    for g in range(1, len(POOL_WINDOWS)):
        total = jnp.where(lane_group == g, sums[g], total)
    y = total / cnt.astype(F32) - z
    return _dot(y.astype(BF16), wpool_bd) * b_scale


def _fill_halo(zh_ref, z_ref, t0, n_tok, seq_len):
    zh_ref[POOL_HALO:POOL_HALO + n_tok, :] = z_ref[pl.ds(t0, n_tok), :]
    lo = jnp.maximum(t0 - POOL_HALO, 0)
    hi = jnp.minimum(t0 + n_tok, seq_len - POOL_HALO)
    lo = pl.multiple_of(lo, POOL_HALO)
    hi = pl.multiple_of(hi, POOL_HALO)
    before = z_ref[pl.ds(lo, POOL_HALO), :]
    after = z_ref[pl.ds(hi, POOL_HALO), :]
    zh_ref[0:POOL_HALO, :] = jnp.where(t0 > 0, before, 0.0)
    zh_ref[POOL_HALO + n_tok:, :] = jnp.where(t0 + n_tok < seq_len, after, 0.0)


def _attend(q_h, k_h, v_h, bias_t, kc_h, vc_h):
    s = _dot_nt(k_h, q_h)
    if bias_t is not None:
        s = s + bias_t
    m = jnp.max(s, axis=0, keepdims=True)
    if kc_h is not None:
        sc = _dot_nt(kc_h, q_h)
        m = jnp.maximum(m, jnp.max(sc, axis=0, keepdims=True))
        pc = jnp.exp(sc - m)
    p = jnp.exp(s - m)
    den = jnp.sum(p, axis=0, keepdims=True)
    o_t = _dot_tn(v_h, p.astype(BF16))
    if kc_h is not None:
        den = den + jnp.sum(pc, axis=0, keepdims=True)
        o_t = o_t + _dot_tn(vc_h, pc.astype(BF16))
    return (o_t / den).T


def _mix_kernel(x_ref, uv_ref, z_ref, q_ref, k_ref, v_ref, kc_ref, vc_ref, bias_ref, mod_ref,
                vn_g_ref, vn_b_ref, ws_ref, bs_ref, wpool_ref, bscale_ref, wout_ref,
                o_ref, zh_ref, mix_ref, *, seq_len):
    gate1 = _mod_chunks(mod_ref, False)[2]
    j = pl.program_id(1)
    n_tok = ROW_BLOCK * GRID_W
    t0 = pl.multiple_of(j * n_tok, n_tok)
    rows = seq_len // GRID_W

    a = _mixer_a(uv_ref[...].astype(F32), vn_g_ref[...], vn_b_ref[...], ws_ref, bs_ref[...])
    mix_ref[:, :A_WIDTH] = a.astype(BF16)

    _fill_halo(zh_ref, z_ref, t0, n_tok, seq_len)
    bmix = _mixer_b(zh_ref, t0, n_tok, seq_len, wpool_ref[...], bscale_ref[...])
    mix_ref[:, A_WIDTH:A_WIDTH + B_WIDTH] = bmix.astype(BF16)

    win_row = jnp.clip(j * ROW_BLOCK - NA_ROWS // 2, 0, rows - KEY_ROWS)
    koff = pl.multiple_of(win_row * GRID_W, GRID_W)
    c_off = A_WIDTH + B_WIDTH
    for h in range(C_HEADS):
        hs = slice(h * HEAD_DIM, (h + 1) * HEAD_DIM)
        o = _attend(q_ref[:, hs],
                    k_ref[pl.ds(koff, KEY_ROWS * GRID_W), hs],
                    v_ref[pl.ds(koff, KEY_ROWS * GRID_W), hs],
                    bias_ref[h], kc_ref[:, hs], vc_ref[:, hs])
        mix_ref[:, c_off + h * HEAD_DIM:c_off + (h + 1) * HEAD_DIM] = o.astype(BF16)

    y = _dot(mix_ref[...], wout_ref[...])
    o_ref[...] = x_ref[...] + gate1 * y


def _mix(x, uv, z, q, k, v, kc, vc, bias, mods, vn_g, vn_b, ws, bs_full, wpool_bd, b_scale, w_out, *, l):
    bsz, seq_len, d = x.shape
    n_ctx = kc.shape[1]
    n_tok = ROW_BLOCK * GRID_W
    n_blocks = seq_len // n_tok
    tok = lambda b, j: (b, j, 0)
    full = lambda b, j: (b, 0, 0)
    c2 = lambda b, j: (0, 0)
    c3 = lambda b, j: (0, 0, 0)

    def bias_map(b, j):
        kind = jnp.where(j == 0, 0, jnp.where(j == n_blocks - 1, 2, 1))
        return (l, kind, 0, 0, 0)

    return pl.pallas_call(
        functools.partial(_mix_kernel, seq_len=seq_len),
        grid=(bsz, n_blocks),
        in_specs=[
            pl.BlockSpec((None, n_tok, d), tok),
            pl.BlockSpec((None, n_tok, 2 * A_WIDTH), tok),
            pl.BlockSpec((None, seq_len, B_WIDTH), full),
            pl.BlockSpec((None, n_tok, C_WIDTH), tok),
            pl.BlockSpec((None, seq_len, C_WIDTH), full),
            pl.BlockSpec((None, seq_len, C_WIDTH), full),
            pl.BlockSpec((None, n_ctx, C_WIDTH), full),
            pl.BlockSpec((None, n_ctx, C_WIDTH), full),
            pl.BlockSpec((None, None, C_HEADS, KEY_ROWS * GRID_W, n_tok), bias_map),
            _mod_spec(l, d),
            _layer_spec(l, (1, A_WIDTH)),
            _layer_spec(l, (1, A_WIDTH)),
            _layer_spec(l, (A_GROUPS, CHUNK, CHUNK)),
            _layer_spec(l, (CHUNK, A_WIDTH)),
            _layer_spec(l, (B_WIDTH, B_WIDTH)),
            _layer_spec(l, (1, B_WIDTH)),
            _layer_spec(l, (d, d)),
        ],
        out_specs=pl.BlockSpec((None, n_tok, d), tok),
        out_shape=jax.ShapeDtypeStruct(x.shape, F32),
        scratch_shapes=[
            pltpu.VMEM((n_tok + 2 * POOL_HALO, B_WIDTH), F32),
            pltpu.VMEM((n_tok, d), BF16),
        ],
        compiler_params=_cparams(("arbitrary", "arbitrary")),
        name="mix",
    )(x, uv, z, q, k, v, kc, vc, bias, mods, vn_g, vn_b, ws, bs_full, wpool_bd, b_scale, w_out)


def _mix_ctx_kernel(x_ref, uv_ref, z_ref, q_ref, k_ref, v_ref, mod_ref,
                    vn_g_ref, vn_b_ref, ws_ref, bs_ref, wpool_ref, bscale_ref, wout_ref,
                    o_ref, zh_ref, mix_ref):
    gate1 = _mod_chunks(mod_ref, True)[2]
    n_tok = x_ref.shape[0]
    a = _mixer_a(uv_ref[...].astype(F32), vn_g_ref[...], vn_b_ref[...], ws_ref, bs_ref[...])
    mix_ref[:, :A_WIDTH] = a.astype(BF16)

    zero = jnp.zeros((POOL_HALO, B_WIDTH), F32)
    zh_ref[0:POOL_HALO, :] = zero
    zh_ref[POOL_HALO:POOL_HALO + n_tok, :] = z_ref[...]
    zh_ref[POOL_HALO + n_tok:, :] = zero
    bmix = _mixer_b(zh_ref, 0, n_tok, n_tok, wpool_ref[...], bscale_ref[...])
    mix_ref[:, A_WIDTH:A_WIDTH + B_WIDTH] = bmix.astype(BF16)

    c_off = A_WIDTH + B_WIDTH
    for h in range(C_HEADS):
        hs = slice(h * HEAD_DIM, (h + 1) * HEAD_DIM)
        o = _attend(q_ref[:, hs], k_ref[:, hs], v_ref[:, hs], None, None, None)
        mix_ref[:, c_off + h * HEAD_DIM:c_off + (h + 1) * HEAD_DIM] = o.astype(BF16)

    y = _dot(mix_ref[...], wout_ref[...])
    o_ref[...] = x_ref[...] + gate1 * y


def _mix_ctx(x, uv, z, q, k, v, mods, vn_g, vn_b, ws, bs_full, wpool_bd, b_scale, w_out, *, l):
    bsz, n_tok, d = x.shape
    full = lambda b: (b, 0, 0)
    c2 = lambda b: (0, 0)
    c3 = lambda b: (0, 0, 0)
    return pl.pallas_call(
        _mix_ctx_kernel,
        grid=(bsz,),
        in_specs=[
            pl.BlockSpec((None, n_tok, d), full),
            pl.BlockSpec((None, n_tok, 2 * A_WIDTH), full),
            pl.BlockSpec((None, n_tok, B_WIDTH), full),
            pl.BlockSpec((None, n_tok, C_WIDTH), full),
            pl.BlockSpec((None, n_tok, C_WIDTH), full),
            pl.BlockSpec((None, n_tok, C_WIDTH), full),
            _mod_spec(l, d),
            _layer_spec(l, (1, A_WIDTH)),
            _layer_spec(l, (1, A_WIDTH)),
            _layer_spec(l, (A_GROUPS, CHUNK, CHUNK)),
            _layer_spec(l, (CHUNK, A_WIDTH)),
            _layer_spec(l, (B_WIDTH, B_WIDTH)),
            _layer_spec(l, (1, B_WIDTH)),
            _layer_spec(l, (d, d)),
        ],
        out_specs=pl.BlockSpec((None, n_tok, d), full),
        out_shape=jax.ShapeDtypeStruct(x.shape, F32),
        scratch_shapes=[
            pltpu.VMEM((n_tok + 2 * POOL_HALO, B_WIDTH), F32),
            pltpu.VMEM((n_tok, d), BF16),
        ],
        compiler_params=_cparams(("arbitrary",)),
        name="mix_ctx",
    )(x, uv, z, q, k, v, mods, vn_g, vn_b, ws, bs_full, wpool_bd, b_scale, w_out)


def _ffn_kernel(x_ref, mod_ref, g_ref, wg_ref, wu_ref, wd_ref, o_ref, h_ref, acc_ref, *, is_ctx):
    f = pl.program_id(2)
    _, _, _, shift, scale, gate = _mod_chunks(mod_ref, is_ctx)

    @pl.when(f == 0)
    def _():
        h = _norm_mod(x_ref[...], g_ref[...], scale, shift)
        h_ref[...] = h.astype(BF16)
        acc_ref[...] = jnp.zeros_like(acc_ref)

    h = h_ref[...]
    act = _silu(_dot(h, wg_ref[...].astype(BF16))) * _dot(h, wu_ref[...].astype(BF16))
    acc_ref[...] += _dot(act.astype(BF16), wd_ref[...].astype(BF16))

    @pl.when(f == pl.num_programs(2) - 1)
    def _():
        o_ref[...] = x_ref[...] + gate * acc_ref[...]


def _ffn(x, mods, g2, w_gate, w_up, w_down, *, l, fi, is_ctx, tm, tf):
    bsz, n_tok, d = x.shape
    d_ff = w_gate.shape[-1]
    tok = lambda b, i, f: (b, i, 0)
    return pl.pallas_call(
        functools.partial(_ffn_kernel, is_ctx=is_ctx),
        grid=(bsz, n_tok // tm, d_ff // tf),
        in_specs=[
            pl.BlockSpec((None, tm, d), tok),
            _mod_spec(l, d),
            _layer_spec(l, (1, d)),
            pl.BlockSpec((None, d, tf), lambda b, i, f: (fi, 0, f)),
            pl.BlockSpec((None, d, tf), lambda b, i, f: (fi, 0, f)),
            pl.BlockSpec((None, tf, d), lambda b, i, f: (fi, f, 0)),
        ],
        out_specs=pl.BlockSpec((None, tm, d), tok),
        out_shape=jax.ShapeDtypeStruct(x.shape, F32),
        scratch_shapes=[pltpu.VMEM((tm, d), BF16), pltpu.VMEM((tm, d), F32)],
        compiler_params=_cparams(("arbitrary", "arbitrary", "arbitrary")),
        name="ffn_ctx" if is_ctx else "ffn",
    )(x, mods, g2, w_gate, w_up, w_down)


def _route_kernel(x_ref, mod_ref, g_ref, wr_ref, h_ref, meta_ref, cnt_ref, run_ref):
    @pl.when((pl.program_id(0) == 0) & (pl.program_id(1) == 0))
    def _():
        run_ref[...] = jnp.zeros_like(run_ref)

    _, _, _, shift, scale, _ = _mod_chunks(mod_ref, False)
    h = _norm_mod(x_ref[...], g_ref[...], scale, shift)
    h_ref[...] = h
    logits = jnp.dot(h, wr_ref[...], preferred_element_type=F32, precision=lax.Precision.HIGHEST)
    tm = logits.shape[0]
    lane = lax.broadcasted_iota(I32, logits.shape, 1)
    logits = jnp.where(lane < N_EXPERTS, logits, -jnp.inf)
    m1 = jnp.max(logits, axis=-1, keepdims=True)
    i1 = jnp.min(jnp.where(logits == m1, lane, META_LANES), axis=-1, keepdims=True)
    rest = jnp.where(lane == i1, -jnp.inf, logits)
    m2 = jnp.max(rest, axis=-1, keepdims=True)
    i2 = jnp.min(jnp.where(rest == m2, lane, META_LANES), axis=-1, keepdims=True)
    e2 = jnp.exp(m2 - m1)
    den = 1.0 + e2

    pick1 = lane == i1
    pick2 = lane == i2
    onehot = jnp.where(pick1, 1.0, 0.0) + jnp.where(pick2, 1.0, 0.0)
    earlier = (lax.broadcasted_iota(I32, (tm, tm), 0) > lax.broadcasted_iota(I32, (tm, tm), 1))
    before = run_ref[...] + _dot(jnp.where(earlier, 1.0, 0.0).astype(BF16), onehot.astype(BF16))
    r1 = jnp.sum(jnp.where(pick1, before, 0.0), axis=-1, keepdims=True)
    r2 = jnp.sum(jnp.where(pick2, before, 0.0), axis=-1, keepdims=True)
    run_ref[...] += jnp.sum(onehot, axis=0, keepdims=True)
    cnt_ref[...] = run_ref[...]

    meta = jnp.zeros(logits.shape, F32)
    for col, val in ((META_E, i1.astype(F32)), (META_E + 1, i2.astype(F32)),
                     (META_G, 1.0 / den), (META_G + 1, e2 / den),
                     (META_RANK, r1), (META_RANK + 1, r2)):
        meta = jnp.where(lane == col, val, meta)
    meta_ref[...] = meta


def _route(x, mods, g2, w_router_pad, *, l, fi, tm):
    bsz, n_tok, d = x.shape
    tok = lambda b, i: (b, i, 0)
    return pl.pallas_call(
        _route_kernel,
        grid=(bsz, n_tok // tm),
        in_specs=[
            pl.BlockSpec((None, tm, d), tok),
            _mod_spec(l, d),
            _layer_spec(l, (1, d)),
            _layer_spec(fi, (d, META_LANES)),
        ],
        out_specs=[
            pl.BlockSpec((None, tm, d), tok),
            pl.BlockSpec((None, tm, META_LANES), tok),
            pl.BlockSpec((1, META_LANES), lambda b, i: (0, 0)),
        ],
        out_shape=[
            jax.ShapeDtypeStruct((bsz, n_tok, d), F32),
            jax.ShapeDtypeStruct((bsz, n_tok, META_LANES), F32),
            jax.ShapeDtypeStruct((1, META_LANES), F32),
        ],
        scratch_shapes=[pltpu.VMEM((1, META_LANES), F32)],
        compiler_params=_cparams(("arbitrary", "arbitrary")),
        name="route",
    )(x, mods, g2, w_router_pad)


def _row_copy(src_ref, src_row, dst_ref, dst_row, sem):
    return pltpu.make_async_copy(src_ref.at[pl.ds(src_row, 1), :], dst_ref.at[pl.ds(dst_row, 1), :], sem)


def _dispatch_kernel(pos_ref, pad_start_ref, pad_count_ref, free_tile_ref, h_ref, xs_ref,
                     zero_ref, sem, zsem):
    tm = h_ref.shape[0]
    step = pl.program_id(0)

    @pl.when(step == 0)
    def _():
        zero_ref[...] = jnp.zeros_like(zero_ref)

        def tile_copy(t):
            return pltpu.make_async_copy(zero_ref, xs_ref.at[pl.ds(t * tm, tm), :], zsem)

        def fill_tile(t, carry):
            tile_copy(t).start()
            return carry

        def drain_tile(t, carry):
            tile_copy(t).wait()
            return carry

        n_tiles = xs_ref.shape[0] // tm
        lax.fori_loop(free_tile_ref[0], n_tiles, fill_tile, 0)
        lax.fori_loop(free_tile_ref[0], n_tiles, drain_tile, 0)
        for e in range(N_EXPERTS):
            start, count = pad_start_ref[e], pad_count_ref[e]
            head = (-start) & (SUBLANES - 1)
            chunks = [(i < head, _row_copy(zero_ref, 0, xs_ref, start + i, zsem))
                      for i in range(SUBLANES - 1)]
            body_start, body = start + head, count - head
            for bit in reversed(range(SUBLANES.bit_length() - 1, (MOE_TM - 1).bit_length())):
                size = 1 << bit
                assert size <= tm
                offset = pl.multiple_of(body_start + (body - (body & (2 * size - 1))), SUBLANES)
                copy = pltpu.make_async_copy(zero_ref.at[pl.ds(0, size), :],
                                             xs_ref.at[pl.ds(offset, size), :], zsem)
                chunks.append(((body & size) != 0, copy))
            for present, copy in chunks:
                pl.when(present)(copy.start)
            for present, copy in chunks:
                pl.when(present)(copy.wait)

    base = step * tm

    def send(r, carry):
        for s in range(TOP_K):
            _row_copy(h_ref, r, xs_ref, pos_ref[TOP_K * (base + r) + s], sem).start()
        return carry

    def done(r, carry):
        for s in range(TOP_K):
            _row_copy(h_ref, r, xs_ref, pos_ref[TOP_K * (base + r) + s], sem).wait()
        return carry

    lax.fori_loop(0, tm, send, 0, unroll=DMA_UNROLL)
    lax.fori_loop(0, tm, done, 0, unroll=DMA_UNROLL)


def _dispatch(h2, pos, pad_start, pad_count, free_tile, n_rows, *, tm):
    n_tok, d = h2.shape
    return pl.pallas_call(
        _dispatch_kernel,
        grid_spec=pltpu.PrefetchScalarGridSpec(
            num_scalar_prefetch=4,
            grid=(n_tok // tm,),
            in_specs=[pl.BlockSpec((tm, d), lambda i, *_: (i, 0))],
            out_specs=pl.BlockSpec(memory_space=pl.ANY),
            scratch_shapes=[pltpu.VMEM((tm, d), F32), pltpu.SemaphoreType.DMA(()),
                            pltpu.SemaphoreType.DMA(())],
        ),
        out_shape=jax.ShapeDtypeStruct((n_rows, d), F32),
        compiler_params=_cparams(("arbitrary",), has_side_effects=True),
        name="dispatch",
    )(pos, pad_start, pad_count, free_tile, h2)


def _gmoe_kernel(tile_expert_ref, n_used_ref, xs_ref, wg_ref, wu_ref, wd_ref, ys_ref, h_ref):
    t = pl.program_id(0)
    f = pl.program_id(1)
    active = t < n_used_ref[0]

    @pl.when(active & (f == 0))
    def _():
        h_ref[...] = xs_ref[...].astype(BF16)

    @pl.when(active)
    def _():
        h = h_ref[...]
        act = _silu(_dot(h, wg_ref[...].astype(BF16))) * _dot(h, wu_ref[...].astype(BF16))
        y = _dot(act.astype(BF16), wd_ref[...].astype(BF16))

        @pl.when(f == 0)
        def _():
            ys_ref[...] = y

        @pl.when(f > 0)
        def _():
            ys_ref[...] += y

    @pl.when(jnp.logical_not(active) & (f == 0))
    def _():
        ys_ref[...] = jnp.zeros_like(ys_ref)


def _gmoe(xs, tile_expert, n_used, w_gate, w_up, w_down, *, fi, tm, tf):
    n_rows, d = xs.shape
    d_ff = w_gate.shape[-1]
    nf = d_ff // tf

    def x_map(t, f, te, nu):
        return (jnp.maximum(jnp.minimum(t, nu[0] - 1), 0), 0)

    def f_idx(t, f, nu):
        return jnp.where(t < nu[0], f, nf - 1)

    return pl.pallas_call(
        _gmoe_kernel,
        grid_spec=pltpu.PrefetchScalarGridSpec(
            num_scalar_prefetch=2,
            grid=(n_rows // tm, nf),
            in_specs=[
                pl.BlockSpec((tm, d), x_map),
                pl.BlockSpec((None, None, d, tf), lambda t, f, te, nu: (fi, te[t], 0, f_idx(t, f, nu))),
                pl.BlockSpec((None, None, d, tf), lambda t, f, te, nu: (fi, te[t], 0, f_idx(t, f, nu))),
                pl.BlockSpec((None, None, tf, d), lambda t, f, te, nu: (fi, te[t], f_idx(t, f, nu), 0)),
            ],
            out_specs=pl.BlockSpec((tm, d), lambda t, f, te, nu: (t, 0)),
            scratch_shapes=[pltpu.VMEM((tm, d), BF16)],
        ),
        out_shape=jax.ShapeDtypeStruct((n_rows, d), F32),
        compiler_params=_cparams(("arbitrary", "arbitrary")),
        name="grouped_moe",
    )(tile_expert, n_used, xs, w_gate, w_up, w_down)


def _combine_kernel(pos_ref, x_ref, meta_ref, mod_ref, ys_ref, o_ref, buf_ref, sem):
    tm = x_ref.shape[0]
    base = (pl.program_id(0) * pl.num_programs(1) + pl.program_id(1)) * tm

    def fetch(r, carry):
        for s in range(TOP_K):
            _row_copy(ys_ref, pos_ref[TOP_K * (base + r) + s], buf_ref.at[s], r, sem).start()
        return carry

    def done(r, carry):
        for s in range(TOP_K):
            _row_copy(ys_ref, pos_ref[TOP_K * (base + r) + s], buf_ref.at[s], r, sem).wait()
        return carry

    lax.fori_loop(0, tm, fetch, 0, unroll=DMA_UNROLL)
    lax.fori_loop(0, tm, done, 0, unroll=DMA_UNROLL)
    meta = meta_ref[...]
    y = meta[:, META_G:META_G + 1] * buf_ref[0] + meta[:, META_G + 1:META_G + 2] * buf_ref[1]
    o_ref[...] = x_ref[...] + _mod_chunks(mod_ref, False)[5] * y


def _combine(x, meta, mods, ys, pos, *, l, tm):
    bsz, n_tok, d = x.shape
    tok = lambda b, i, *_: (b, i, 0)
    return pl.pallas_call(
        _combine_kernel,
        grid_spec=pltpu.PrefetchScalarGridSpec(
            num_scalar_prefetch=1,
            grid=(bsz, n_tok // tm),
            in_specs=[
                pl.BlockSpec((None, tm, d), tok),
                pl.BlockSpec((None, tm, META_LANES), tok),
                _mod_spec(l, d),
                pl.BlockSpec(memory_space=pl.ANY),
            ],
            out_specs=pl.BlockSpec((None, tm, d), tok),
            scratch_shapes=[pltpu.VMEM((TOP_K, tm, d), F32), pltpu.SemaphoreType.DMA(())],
        ),
        out_shape=jax.ShapeDtypeStruct(x.shape, F32),
        compiler_params=_cparams(("arbitrary", "arbitrary")),
        name="combine",
    )(pos, x, meta, mods, ys)


def _moe(x, mods, g2, w_router, w_gate, w_up, w_down, *, l, fi):
    bsz, n_tok, d = x.shape
    tm = MOE_TM
    n_all = bsz * n_tok
    n_tiles = TOP_K * n_all // tm + N_EXPERTS
    wr = jnp.pad(w_router, ((0, 0), (0, 0), (0, META_LANES - N_EXPERTS)))
    h2, meta, counts = _route(x, mods, g2, wr, l=l, fi=fi, tm=ROUTE_TM)

    meta2 = meta.reshape(n_all, META_LANES)
    expert = meta2[:, META_E:META_E + TOP_K].astype(I32)
    rank = meta2[:, META_RANK:META_RANK + TOP_K].astype(I32)
    cnt = counts[0, :N_EXPERTS].astype(I32)
    tiles = (cnt + tm - 1) // tm
    tile_end = jnp.cumsum(tiles)
    row_start = (tile_end - tiles) * tm
    expert_ids = jnp.arange(N_EXPERTS, dtype=I32)
    pos = (rank + jnp.sum(jnp.where(expert[..., None] == expert_ids, row_start, 0), axis=-1)).reshape(-1)
    n_used = tile_end[-1:]
    tile_ids = jnp.arange(n_tiles, dtype=I32)
    tile_expert = jnp.sum(tile_ids[:, None] >= tile_end[None, :], axis=1).astype(I32)
    last_expert = jnp.max(jnp.where(tiles > 0, expert_ids, 0))
    tile_expert = jnp.minimum(tile_expert, last_expert)
    pad_start = row_start + cnt
    pad_count = tiles * tm - cnt

    free_tile = n_used * (tm // ROUTE_TM)
    xs = _dispatch(h2.reshape(n_all, d), pos, pad_start, pad_count, free_tile, n_tiles * tm, tm=ROUTE_TM)
    ys = _gmoe(xs, tile_expert, n_used, w_gate, w_up, w_down, fi=fi, tm=tm, tf=512)
    return _combine(x, meta, mods, ys, pos, l=l, tm=ROUTE_TM)


N_ROW_OFF = 2 * NA_ROWS - 1
N_COL_OFF = 2 * NA_COLS - 1


def _bias_cells(rows):
    n_blocks = rows // ROW_BLOCK
    row_off = np.zeros((3, ROW_BLOCK, KEY_ROWS), np.int64)
    row_in = np.zeros((3, ROW_BLOCK, KEY_ROWS), bool)
    for kind, blk in enumerate((0, 1, n_blocks - 1)):
        win = int(np.clip(blk * ROW_BLOCK - NA_ROWS // 2, 0, rows - KEY_ROWS))
        r = blk * ROW_BLOCK + np.arange(ROW_BLOCK)[:, None]
        kr = win + np.arange(KEY_ROWS)[None, :]
        first = np.clip(r - NA_ROWS // 2, 0, rows - NA_ROWS)
        row_in[kind] = (kr >= first) & (kr < first + NA_ROWS)
        row_off[kind] = np.clip(kr - r + NA_ROWS - 1, 0, N_ROW_OFF - 1)
    return row_off, row_in


def _bias_kernel(rpb_ref, o_ref, toep_ref, *, rows):
    l, h, kind = pl.program_id(0), pl.program_id(1), pl.program_id(2)
    shape = (GRID_W, 2 * GRID_W)
    lane = lax.broadcasted_iota(I32, shape, 1)

    @pl.when(kind == 0)
    def _():
        kc = lax.broadcasted_iota(I32, shape, 0)
        qc = lane % GRID_W
        start = jnp.clip(qc - NA_COLS // 2, 0, GRID_W - NA_COLS)
        col_in = (kc >= start) & (kc < start + NA_COLS)
        col_off = jnp.clip(kc - qc, -(NA_COLS - 1), NA_COLS - 1) + (NA_COLS - 1)
        base = (l * C_HEADS + h) * (N_ROW_OFF * N_COL_OFF)
        for dr in range(N_ROW_OFF):
            t = jnp.full(shape, NEG_INF, F32)
            for dc in range(N_COL_OFF):
                t = jnp.where(col_off == dc, rpb_ref[base + dr * N_COL_OFF + dc], t)
            toep_ref[dr] = jnp.where(col_in, t, NEG_INF)

    row_off, row_in = _bias_cells(rows)
    masked = jnp.full(shape, NEG_INF, F32)
    for k in range(3):
        @pl.when(kind == k)
        def _():
            for jj in range(KEY_ROWS):
                for ip in range(ROW_BLOCK // 2):
                    halves = [toep_ref[int(row_off[k, i, jj])] if row_in[k, i, jj] else masked
                              for i in (2 * ip, 2 * ip + 1)]
                    o_ref[jj * GRID_W:(jj + 1) * GRID_W, ip * 2 * GRID_W:(ip + 1) * 2 * GRID_W] = (
                        jnp.where(lane < GRID_W, halves[0], halves[1]))


def _attention_bias(rpb, rows):
    depth = rpb.shape[0]
    assert ROW_BLOCK % 2 == 0
    return pl.pallas_call(
        functools.partial(_bias_kernel, rows=rows),
        grid=(depth, C_HEADS, 3),
        in_specs=[pl.BlockSpec(memory_space=pltpu.SMEM)],
        out_specs=pl.BlockSpec((None, None, None, KEY_ROWS * GRID_W, ROW_BLOCK * GRID_W),
                               lambda l, h, k: (l, k, h, 0, 0)),
        out_shape=jax.ShapeDtypeStruct(
            (depth, 3, C_HEADS, KEY_ROWS * GRID_W, ROW_BLOCK * GRID_W), F32),
        scratch_shapes=[pltpu.VMEM((N_ROW_OFF, GRID_W, 2 * GRID_W), F32)],
        compiler_params=_cparams(("arbitrary", "arbitrary", "arbitrary")),
        name="attention_bias",
    )(rpb.astype(F32).reshape(-1))


def _block_diag(blocks):
    g, m, n = blocks.shape[-3:]
    eye = jnp.eye(g, dtype=blocks.dtype)
    out = blocks[..., :, :, None, :] * eye[:, None, :, None]
    return out.reshape(blocks.shape[:-3] + (g * m, g * n))


def kernel(x, c, ctx, c_ctx, w_mod, b_mod, norm1_g, norm2_g, w_in, w_out, a_vn_g, a_vn_b, a_ws, a_bs,
           b_wpool, b_scale, c_qn_g, c_kn_g, c_rpb, ffn_w_gate, ffn_w_up, ffn_w_down,
           moe_w_router, moe_w_gate, moe_w_up, moe_w_down):
    bsz, seq_len, d = x.shape
    depth = w_mod.shape[0]
    rows = seq_len // GRID_W
    assert bsz <= CTX_MOD_ROW

    cond = jnp.concatenate([c, jnp.zeros((CTX_MOD_ROW - bsz, d), F32), c_ctx[None, :],
                            jnp.zeros((MOD_ROWS - CTX_MOD_ROW - 1, d), F32)], axis=0)
    mods = _modulation(cond, w_mod, b_mod)
    hsum = jnp.asarray(np.kron(np.eye(C_HEADS), np.full((HEAD_DIM, HEAD_DIM), 1.0 / HEAD_DIM)), BF16)

    g1 = norm1_g.reshape(depth, 1, d)
    g2 = norm2_g.reshape(depth, 1, d)
    w_in_b = w_in.astype(BF16)
    qg = (jnp.tile(c_qn_g, (1, C_HEADS)) * ATTN_SCALE).reshape(depth, 1, C_WIDTH)
    kg = jnp.tile(c_kn_g, (1, C_HEADS)).reshape(depth, 1, C_WIDTH)
    mix_w = (a_vn_g.reshape(depth, 1, A_WIDTH), a_vn_b.reshape(depth, 1, A_WIDTH),
             a_ws.astype(BF16),
             jnp.repeat(jnp.swapaxes(a_bs, 1, 2), HEAD_DIM, axis=2),
             _block_diag(b_wpool).astype(BF16),
             b_scale.reshape(depth, 1, B_WIDTH),
             w_out.astype(BF16))
    bias = _attention_bias(c_rpb, rows)

    xc = ctx
    for l in range(depth):
        last = l == depth - 1
        fi = l // 2
        uv_c, z_c, q_c, k_c, v_c = _in_proj(xc, mods, g1, w_in_b, hsum, qg, kg,
                                            l=l, is_ctx=True, tm=xc.shape[1])
        uv, z, q, k, v = _in_proj(x, mods, g1, w_in_b, hsum, qg, kg, l=l, is_ctx=False, tm=512)
        x = _mix(x, uv, z, q, k, v, k_c, v_c, bias, mods, *mix_w, l=l)

        ffn_w = (ffn_w_gate, ffn_w_up, ffn_w_down)
        if l % 2 == 0:
            x = _ffn(x, mods, g2, *ffn_w, l=l, fi=fi, is_ctx=False, tm=1024, tf=256)
        else:
            x = _moe(x, mods, g2, moe_w_router, moe_w_gate, moe_w_up, moe_w_down, l=l, fi=fi)

        if not last:
            xc = _mix_ctx(xc, uv_c, z_c, q_c, k_c, v_c, mods, *mix_w, l=l)
            n_ctx = xc.shape[1]
            if l % 2 == 0:
                xc = _ffn(xc.reshape(1, bsz * n_ctx, d), mods, g2, *ffn_w, l=l, fi=fi,
                          is_ctx=True, tm=bsz * n_ctx, tf=256).reshape(bsz, n_ctx, d)
            else:
                raise NotImplementedError("a context-stream MoE layer only occurs for depth > 2")
    return x
```

```python
import functools

import numpy as np
import jax
import jax.numpy as jnp
from jax import lax
from jax.experimental import pallas as pl
from jax.experimental.pallas import tpu as pltpu

F32 = jnp.float32
BF16 = jnp.bfloat16
I32 = jnp.int32

D_MODEL = 1024
GRID_W = 64
HEAD_DIM = 64
A_WIDTH = 256
A_GROUPS = 4
CHUNK = 128
B_WIDTH = 256
POOL_WINDOWS = (2, 4, 8, 16)
POOL_HALO = 8
C_WIDTH = 512
C_HEADS = 8
NA_ROWS = 8
NA_COLS = 16
ATTN_SCALE = HEAD_DIM ** -0.5
IN_COLS = 2 * A_WIDTH + B_WIDTH + 3 * C_WIDTH
N_EXPERTS = 8
TOP_K = 2
EPS = 1e-6
NEG_INF = -1e30

ROW_BLOCK = 4
KEY_ROWS = ROW_BLOCK + NA_ROWS
assert ROW_BLOCK >= NA_ROWS // 2
MOD_ROWS = 8
CTX_MOD_ROW = 4
META_LANES = 128
META_E, META_G, META_RANK = 0, 2, 4
MOE_TM = 1024
ROUTE_TM = 512
DMA_UNROLL = 8
SUBLANES = 8

VMEM_LIMIT = 60 * 1024 * 1024


def _cparams(sem, **kw):
    return pltpu.CompilerParams(dimension_semantics=sem, vmem_limit_bytes=VMEM_LIMIT, **kw)


def _dot(a, b):
    return jnp.dot(a, b, preferred_element_type=F32)


def _silu(t):
    return t / (1.0 + jnp.exp(-t))


def _gelu(t):
    return 0.5 * t * (1.0 + lax.erf(t * np.float32(np.sqrt(0.5))))


def _norm_mod(x, g, scale, shift):
    y = x * lax.rsqrt(jnp.mean(x * x, axis=-1, keepdims=True) + EPS) * g
    return y * (1.0 + scale) + shift


def _mod_chunks(mod_ref, is_ctx):
    row = CTX_MOD_ROW if is_ctx else pl.program_id(0)
    m = mod_ref[pl.ds(row, 1), :]
    d = m.shape[1] // 6
    return [m[:, k * d:(k + 1) * d] for k in range(6)]


def _mod_spec(l, d):
    return pl.BlockSpec((None, MOD_ROWS, 6 * d), lambda *_: (l, 0, 0))


def _layer_spec(l, shape):
    zeros = (0,) * len(shape)
    return pl.BlockSpec((None,) + tuple(shape), lambda *_: (l,) + zeros)


def _mod_kernel(cond_ref, w_ref, b_ref, o_ref):
    s = _silu(cond_ref[...])
    o_ref[...] = _dot(s.astype(BF16), w_ref[...].astype(BF16)) + b_ref[...]


def _modulation(cond, w_mod, b_mod):
    depth, d, cols = w_mod.shape
    tn = 1536
    return pl.pallas_call(
        _mod_kernel,
        grid=(depth, cols // tn),
        in_specs=[
            pl.BlockSpec((MOD_ROWS, d), lambda l, j: (0, 0)),
            pl.BlockSpec((None, d, tn), lambda l, j: (l, 0, j)),
            pl.BlockSpec((None, 1, tn), lambda l, j: (l, 0, j)),
        ],
        out_specs=pl.BlockSpec((None, MOD_ROWS, tn), lambda l, j: (l, 0, j)),
        out_shape=jax.ShapeDtypeStruct((depth, MOD_ROWS, cols), F32),
        compiler_params=_cparams(("arbitrary", "arbitrary")),
        name="modulation",
    )(cond, w_mod, b_mod.reshape(depth, 1, cols))


def _head_rms(t, hsum):
    ms = _dot((t * t).astype(BF16), hsum)
    return t * lax.rsqrt(ms + EPS)


def _in_proj_kernel(x_ref, mod_ref, g_ref, w_ref, hs_ref, qg_ref, kg_ref,
                    uv_ref, z_ref, q_ref, kt_ref, v_ref, *, is_ctx):
    shift, scale = _mod_chunks(mod_ref, is_ctx)[:2]
    h = _norm_mod(x_ref[...], g_ref[...], scale, shift)
    p = _dot(h.astype(BF16), w_ref[...])
    o = 2 * A_WIDTH
    uv_ref[...] = p[:, :o].astype(BF16)
    z_ref[...] = p[:, o:o + B_WIDTH]
    o += B_WIDTH
    hs = hs_ref[...]
    q_ref[...] = (_head_rms(p[:, o:o + C_WIDTH], hs) * qg_ref[...]).astype(BF16)
    o += C_WIDTH
    kt_ref[...] = (_head_rms(p[:, o:o + C_WIDTH], hs) * kg_ref[...]).T.astype(BF16)
    o += C_WIDTH
    v_ref[...] = p[:, o:o + C_WIDTH].astype(BF16)


def _in_proj(x, mods, g1, w_in, hsum, qg, kg, *, l, is_ctx, tm):
    bsz, n_tok, d = x.shape
    const = lambda b, i: (0, 0)
    tok = lambda b, i: (b, i, 0)

    def out(width, dtype):
        return (pl.BlockSpec((None, tm, width), tok),
                jax.ShapeDtypeStruct((bsz, n_tok, width), dtype))

    keys_t = (pl.BlockSpec((None, C_WIDTH, tm), lambda b, i: (b, 0, i)),
              jax.ShapeDtypeStruct((bsz, C_WIDTH, n_tok), BF16))
    outs = [out(2 * A_WIDTH, BF16), out(B_WIDTH, F32), out(C_WIDTH, BF16),
            keys_t, out(C_WIDTH, BF16)]
    return pl.pallas_call(
        functools.partial(_in_proj_kernel, is_ctx=is_ctx),
        grid=(bsz, n_tok // tm),
        in_specs=[
            pl.BlockSpec((None, tm, d), tok),
            _mod_spec(l, d),
            _layer_spec(l, (1, d)),
            _layer_spec(l, (d, IN_COLS)),
            pl.BlockSpec((C_WIDTH, C_WIDTH), const),
            _layer_spec(l, (1, C_WIDTH)),
            _layer_spec(l, (1, C_WIDTH)),
        ],
        out_specs=[o[0] for o in outs],
        out_shape=[o[1] for o in outs],
        compiler_params=_cparams(("arbitrary", "arbitrary")),
        name="in_proj_ctx" if is_ctx else "in_proj",
    )(x, mods, g1, w_in, hsum, qg, kg)


def _mixer_a(uv, vn_g, vn_b, ws_ref, bs_full):
    n_tok = uv.shape[0]
    u = _gelu(uv[:, :A_WIDTH])
    v = _gelu(uv[:, A_WIDTH:])
    vc = v - jnp.mean(v, axis=-1, keepdims=True)
    var = jnp.mean(vc * vc, axis=-1, keepdims=True)
    v = (vc * lax.rsqrt(var + EPS) * vn_g + vn_b).astype(BF16)
    lane_group = lax.broadcasted_iota(I32, (CHUNK, A_WIDTH), 1) // HEAD_DIM
    outs = []
    for c in range(n_tok // CHUNK):
        v_c = v[c * CHUNK:(c + 1) * CHUNK]
        mixed = bs_full
        for g in range(A_GROUPS):
            mixed = mixed + jnp.where(lane_group == g, _dot(ws_ref[g], v_c), 0.0)
        outs.append(u[c * CHUNK:(c + 1) * CHUNK] * mixed)
    return jnp.concatenate(outs, axis=0) if len(outs) > 1 else outs[0]


def _mixer_b(zh_ref, t0, n_tok, seq_len, wpool_bd, b_scale):
    def sh(d):
        return zh_ref[POOL_HALO + d:POOL_HALO + d + n_tok, :]

    z = sh(0)
    s = sh(-1) + z
    sums = [s]
    for win in POOL_WINDOWS[1:]:
        half = win // 2
        for d in range(-half, -half // 2):
            s = s + sh(d)
        for d in range(half // 2, half):
            s = s + sh(d)
        sums.append(s)
    lane_group = lax.broadcasted_iota(I32, (n_tok, B_WIDTH), 1) // (B_WIDTH // len(POOL_WINDOWS))
    pos = t0 + lax.broadcasted_iota(I32, (n_tok, B_WIDTH), 0)
    half = jnp.left_shift(1, lane_group)
    cnt = jnp.minimum(pos + half, seq_len) - jnp.maximum(pos - half, 0)
    total = sums[0]
    for g in range(1, len(POOL_WINDOWS)):
        total = jnp.where(lane_group == g, sums[g], total)
    y = total / cnt.astype(F32) - z
    return _dot(y.astype(BF16), wpool_bd) * b_scale


def _fill_halo(zh_ref, z_ref, t0, n_tok, seq_len):
    zh_ref[POOL_HALO:POOL_HALO + n_tok, :] = z_ref[pl.ds(t0, n_tok), :]
    lo = jnp.maximum(t0 - POOL_HALO, 0)
    hi = jnp.minimum(t0 + n_tok, seq_len - POOL_HALO)
    lo = pl.multiple_of(lo, POOL_HALO)
    hi = pl.multiple_of(hi, POOL_HALO)
    before = z_ref[pl.ds(lo, POOL_HALO), :]
    after = z_ref[pl.ds(hi, POOL_HALO), :]
    zh_ref[0:POOL_HALO, :] = jnp.where(t0 > 0, before, 0.0)
    zh_ref[POOL_HALO + n_tok:, :] = jnp.where(t0 + n_tok < seq_len, after, 0.0)


def _attend(q_h, kt_h, v_h, bias_h, kct_h, vc_h):
    s = _dot(q_h, kt_h)
    if bias_h is not None:
        s = s + bias_h
    m = jnp.max(s, axis=-1, keepdims=True)
    if kct_h is not None:
        sc = _dot(q_h, kct_h)
        m = jnp.maximum(m, jnp.max(sc, axis=-1, keepdims=True))
        pc = jnp.exp(sc - m)
    p = jnp.exp(s - m)
    den = jnp.sum(p, axis=-1, keepdims=True)
    o = _dot(p.astype(BF16), v_h)
    if kct_h is not None:
        den = den + jnp.sum(pc, axis=-1, keepdims=True)
        o = o + _dot(pc.astype(BF16), vc_h)
    return o / den


def _mix_kernel(x_ref, uv_ref, z_ref, q_ref, kt_ref, v_ref, kct_ref, vc_ref, bias_ref, mod_ref,
                vn_g_ref, vn_b_ref, ws_ref, bs_ref, wpool_ref, bscale_ref, wout_ref,
                o_ref, zh_ref, mix_ref, *, seq_len):
    gate1 = _mod_chunks(mod_ref, False)[2]
    j = pl.program_id(1)
    n_tok = ROW_BLOCK * GRID_W
    t0 = pl.multiple_of(j * n_tok, n_tok)
    rows = seq_len // GRID_W

    a = _mixer_a(uv_ref[...].astype(F32), vn_g_ref[...], vn_b_ref[...], ws_ref, bs_ref[...])
    mix_ref[:, :A_WIDTH] = a.astype(BF16)

    _fill_halo(zh_ref, z_ref, t0, n_tok, seq_len)
    bmix = _mixer_b(zh_ref, t0, n_tok, seq_len, wpool_ref[...], bscale_ref[...])
    mix_ref[:, A_WIDTH:A_WIDTH + B_WIDTH] = bmix.astype(BF16)

    win_row = jnp.clip(j * ROW_BLOCK - NA_ROWS // 2, 0, rows - KEY_ROWS)
    koff = pl.multiple_of(win_row * GRID_W, 2 * GRID_W)
    c_off = A_WIDTH + B_WIDTH
    for h in range(C_HEADS):
        hs = slice(h * HEAD_DIM, (h + 1) * HEAD_DIM)
        o = _attend(q_ref[:, hs],
                    kt_ref[hs, pl.ds(koff, KEY_ROWS * GRID_W)],
                    v_ref[pl.ds(koff, KEY_ROWS * GRID_W), hs],
                    bias_ref[h], kct_ref[hs, :], vc_ref[:, hs])
        mix_ref[:, c_off + h * HEAD_DIM:c_off + (h + 1) * HEAD_DIM] = o.astype(BF16)

    y = _dot(mix_ref[...], wout_ref[...])
    o_ref[...] = x_ref[...] + gate1 * y


def _mix(x, uv, z, q, k, v, kc, vc, bias, mods, vn_g, vn_b, ws, bs_full, wpool_bd, b_scale, w_out, *, l):
    bsz, seq_len, d = x.shape
    n_ctx = vc.shape[1]
    n_tok = ROW_BLOCK * GRID_W
    n_blocks = seq_len // n_tok
    assert ROW_BLOCK % 2 == 0 and (NA_ROWS // 2) % 2 == 0 and (seq_len // GRID_W - KEY_ROWS) % 2 == 0
    tok = lambda b, j: (b, j, 0)
    full = lambda b, j: (b, 0, 0)
    c2 = lambda b, j: (0, 0)
    c3 = lambda b, j: (0, 0, 0)

    def bias_map(b, j):
        kind = jnp.where(j == 0, 0, jnp.where(j == n_blocks - 1, 2, 1))
        return (l, kind, 0, 0, 0)

    return pl.pallas_call(
        functools.partial(_mix_kernel, seq_len=seq_len),
        grid=(bsz, n_blocks),
        in_specs=[
            pl.BlockSpec((None, n_tok, d), tok),
            pl.BlockSpec((None, n_tok, 2 * A_WIDTH), tok),
            pl.BlockSpec((None, seq_len, B_WIDTH), full),
            pl.BlockSpec((None, n_tok, C_WIDTH), tok),
            pl.BlockSpec((None, C_WIDTH, seq_len), full),
            pl.BlockSpec((None, seq_len, C_WIDTH), full),
            pl.BlockSpec((None, C_WIDTH, n_ctx), full),
            pl.BlockSpec((None, n_ctx, C_WIDTH), full),
            pl.BlockSpec((None, None, C_HEADS, n_tok, KEY_ROWS * GRID_W), bias_map),
            _mod_spec(l, d),
            _layer_spec(l, (1, A_WIDTH)),
            _layer_spec(l, (1, A_WIDTH)),
            _layer_spec(l, (A_GROUPS, CHUNK, CHUNK)),
            _layer_spec(l, (CHUNK, A_WIDTH)),
            _layer_spec(l, (B_WIDTH, B_WIDTH)),
            _layer_spec(l, (1, B_WIDTH)),
            _layer_spec(l, (d, d)),
        ],
        out_specs=pl.BlockSpec((None, n_tok, d), tok),
        out_shape=jax.ShapeDtypeStruct(x.shape, F32),
        scratch_shapes=[
            pltpu.VMEM((n_tok + 2 * POOL_HALO, B_WIDTH), F32),
            pltpu.VMEM((n_tok, d), BF16),
        ],
        compiler_params=_cparams(("arbitrary", "arbitrary")),
        name="mix",
    )(x, uv, z, q, k, v, kc, vc, bias, mods, vn_g, vn_b, ws, bs_full, wpool_bd, b_scale, w_out)


def _mix_ctx_kernel(x_ref, uv_ref, z_ref, q_ref, kt_ref, v_ref, mod_ref,
                    vn_g_ref, vn_b_ref, ws_ref, bs_ref, wpool_ref, bscale_ref, wout_ref,
                    o_ref, zh_ref, mix_ref):
    gate1 = _mod_chunks(mod_ref, True)[2]
    n_tok = x_ref.shape[0]
    a = _mixer_a(uv_ref[...].astype(F32), vn_g_ref[...], vn_b_ref[...], ws_ref, bs_ref[...])
    mix_ref[:, :A_WIDTH] = a.astype(BF16)

    zero = jnp.zeros((POOL_HALO, B_WIDTH), F32)
    zh_ref[0:POOL_HALO, :] = zero
    zh_ref[POOL_HALO:POOL_HALO + n_tok, :] = z_ref[...]
    zh_ref[POOL_HALO + n_tok:, :] = zero
    bmix = _mixer_b(zh_ref, 0, n_tok, n_tok, wpool_ref[...], bscale_ref[...])
    mix_ref[:, A_WIDTH:A_WIDTH + B_WIDTH] = bmix.astype(BF16)

    c_off = A_WIDTH + B_WIDTH
    for h in range(C_HEADS):
        hs = slice(h * HEAD_DIM, (h + 1) * HEAD_DIM)
        o = _attend(q_ref[:, hs], kt_ref[hs, :], v_ref[:, hs], None, None, None)
        mix_ref[:, c_off + h * HEAD_DIM:c_off + (h + 1) * HEAD_DIM] = o.astype(BF16)

    y = _dot(mix_ref[...], wout_ref[...])
    o_ref[...] = x_ref[...] + gate1 * y


def _mix_ctx(x, uv, z, q, k, v, mods, vn_g, vn_b, ws, bs_full, wpool_bd, b_scale, w_out, *, l):
    bsz, n_tok, d = x.shape
    full = lambda b: (b, 0, 0)
    c2 = lambda b: (0, 0)
    c3 = lambda b: (0, 0, 0)
    return pl.pallas_call(
        _mix_ctx_kernel,
        grid=(bsz,),
        in_specs=[
            pl.BlockSpec((None, n_tok, d), full),
            pl.BlockSpec((None, n_tok, 2 * A_WIDTH), full),
            pl.BlockSpec((None, n_tok, B_WIDTH), full),
            pl.BlockSpec((None, n_tok, C_WIDTH), full),
            pl.BlockSpec((None, C_WIDTH, n_tok), full),
            pl.BlockSpec((None, n_tok, C_WIDTH), full),
            _mod_spec(l, d),
            _layer_spec(l, (1, A_WIDTH)),
            _layer_spec(l, (1, A_WIDTH)),
            _layer_spec(l, (A_GROUPS, CHUNK, CHUNK)),
            _layer_spec(l, (CHUNK, A_WIDTH)),
            _layer_spec(l, (B_WIDTH, B_WIDTH)),
            _layer_spec(l, (1, B_WIDTH)),
            _layer_spec(l, (d, d)),
        ],
        out_specs=pl.BlockSpec((None, n_tok, d), full),
        out_shape=jax.ShapeDtypeStruct(x.shape, F32),
        scratch_shapes=[
            pltpu.VMEM((n_tok + 2 * POOL_HALO, B_WIDTH), F32),
            pltpu.VMEM((n_tok, d), BF16),
        ],
        compiler_params=_cparams(("arbitrary",)),
        name="mix_ctx",
    )(x, uv, z, q, k, v, mods, vn_g, vn_b, ws, bs_full, wpool_bd, b_scale, w_out)


def _swiglu_up(h_ref, wg_ref, wu_ref, act_ref, slot):
    h = h_ref[...]
    act = _silu(_dot(h, wg_ref[...].astype(BF16))) * _dot(h, wu_ref[...].astype(BF16))
    act_ref[slot] = act.astype(BF16)


def _swiglu_down(act_ref, slot, wd_ref, acc_ref):
    acc_ref[...] += _dot(act_ref[slot], wd_ref[...].astype(BF16))


def _ffn_kernel(x_ref, mod_ref, g_ref, wg_ref, wu_ref, wd_ref, o_ref, h_ref, acc_ref, act_ref,
                *, is_ctx):
    f = pl.program_id(2)
    last = pl.num_programs(2) - 1
    _, _, _, shift, scale, gate = _mod_chunks(mod_ref, is_ctx)

    @pl.when(f == 0)
    def _():
        h = _norm_mod(x_ref[...], g_ref[...], scale, shift)
        h_ref[...] = h.astype(BF16)
        acc_ref[...] = jnp.zeros_like(acc_ref)
        _swiglu_up(h_ref, wg_ref, wu_ref, act_ref, 0)

    @pl.when((f > 0) & (f < last))
    def _():
        _swiglu_down(act_ref, (f - 1) % 2, wd_ref, acc_ref)
        _swiglu_up(h_ref, wg_ref, wu_ref, act_ref, f % 2)

    @pl.when(f == last)
    def _():
        _swiglu_down(act_ref, (f - 1) % 2, wd_ref, acc_ref)
        o_ref[...] = x_ref[...] + gate * acc_ref[...]


def _ffn(x, mods, g2, w_gate, w_up, w_down, *, l, fi, is_ctx, tm, tf):
    bsz, n_tok, d = x.shape
    d_ff = w_gate.shape[-1]
    nf = d_ff // tf
    tok = lambda b, i, f: (b, i, 0)
    up_map = lambda b, i, f: (fi, 0, jnp.minimum(f, nf - 1))
    return pl.pallas_call(
        functools.partial(_ffn_kernel, is_ctx=is_ctx),
        grid=(bsz, n_tok // tm, nf + 1),
        in_specs=[
            pl.BlockSpec((None, tm, d), tok),
            _mod_spec(l, d),
            _layer_spec(l, (1, d)),
            pl.BlockSpec((None, d, tf), up_map),
            pl.BlockSpec((None, d, tf), up_map),
            pl.BlockSpec((None, tf, d), lambda b, i, f: (fi, jnp.maximum(f - 1, 0), 0)),
        ],
        out_specs=pl.BlockSpec((None, tm, d), tok),
        out_shape=jax.ShapeDtypeStruct(x.shape, F32),
        scratch_shapes=[pltpu.VMEM((tm, d), BF16), pltpu.VMEM((tm, d), F32),
                        pltpu.VMEM((2, tm, tf), BF16)],
        compiler_params=_cparams(("arbitrary", "arbitrary", "arbitrary")),
        name="ffn_ctx" if is_ctx else "ffn",
    )(x, mods, g2, w_gate, w_up, w_down)


def _route_kernel(x_ref, mod_ref, g_ref, wr_ref, h_ref, meta_ref, cnt_ref, run_ref):
    @pl.when((pl.program_id(0) == 0) & (pl.program_id(1) == 0))
    def _():
        run_ref[...] = jnp.zeros_like(run_ref)

    _, _, _, shift, scale, _ = _mod_chunks(mod_ref, False)
    h = _norm_mod(x_ref[...], g_ref[...], scale, shift)
    h_ref[...] = h
    logits = jnp.dot(h, wr_ref[...], preferred_element_type=F32, precision=lax.Precision.HIGHEST)
    tm = logits.shape[0]
    lane = lax.broadcasted_iota(I32, logits.shape, 1)
    logits = jnp.where(lane < N_EXPERTS, logits, -jnp.inf)
    m1 = jnp.max(logits, axis=-1, keepdims=True)
    i1 = jnp.min(jnp.where(logits == m1, lane, META_LANES), axis=-1, keepdims=True)
    rest = jnp.where(lane == i1, -jnp.inf, logits)
    m2 = jnp.max(rest, axis=-1, keepdims=True)
    i2 = jnp.min(jnp.where(rest == m2, lane, META_LANES), axis=-1, keepdims=True)
    e2 = jnp.exp(m2 - m1)
    den = 1.0 + e2

    pick1 = lane == i1
    pick2 = lane == i2
    onehot = jnp.where(pick1, 1.0, 0.0) + jnp.where(pick2, 1.0, 0.0)
    earlier = (lax.broadcasted_iota(I32, (tm, tm), 0) > lax.broadcasted_iota(I32, (tm, tm), 1))
    before = run_ref[...] + _dot(jnp.where(earlier, 1.0, 0.0).astype(BF16), onehot.astype(BF16))
    r1 = jnp.sum(jnp.where(pick1, before, 0.0), axis=-1, keepdims=True)
    r2 = jnp.sum(jnp.where(pick2, before, 0.0), axis=-1, keepdims=True)
    run_ref[...] += jnp.sum(onehot, axis=0, keepdims=True)
    cnt_ref[...] = run_ref[...]

    meta = jnp.zeros(logits.shape, F32)
    for col, val in ((META_E, i1.astype(F32)), (META_E + 1, i2.astype(F32)),
                     (META_G, 1.0 / den), (META_G + 1, e2 / den),
                     (META_RANK, r1), (META_RANK + 1, r2)):
        meta = jnp.where(lane == col, val, meta)
    meta_ref[...] = meta


def _route(x, mods, g2, w_router_pad, *, l, fi, tm):
    bsz, n_tok, d = x.shape
    tok = lambda b, i: (b, i, 0)
    return pl.pallas_call(
        _route_kernel,
        grid=(bsz, n_tok // tm),
        in_specs=[
            pl.BlockSpec((None, tm, d), tok),
            _mod_spec(l, d),
            _layer_spec(l, (1, d)),
            _layer_spec(fi, (d, META_LANES)),
        ],
        out_specs=[
            pl.BlockSpec((None, tm, d), tok),
            pl.BlockSpec((None, tm, META_LANES), tok),
            pl.BlockSpec((1, META_LANES), lambda b, i: (0, 0)),
        ],
        out_shape=[
            jax.ShapeDtypeStruct((bsz, n_tok, d), F32),
            jax.ShapeDtypeStruct((bsz, n_tok, META_LANES), F32),
            jax.ShapeDtypeStruct((1, META_LANES), F32),
        ],
        scratch_shapes=[pltpu.VMEM((1, META_LANES), F32)],
        compiler_params=_cparams(("arbitrary", "arbitrary")),
        name="route",
    )(x, mods, g2, w_router_pad)


def _row_copy(src_ref, src_row, dst_ref, dst_row, sem):
    return pltpu.make_async_copy(src_ref.at[pl.ds(src_row, 1), :], dst_ref.at[pl.ds(dst_row, 1), :], sem)


def _dispatch_kernel(pos_ref, pad_start_ref, pad_count_ref, free_tile_ref, h_ref, xs_ref,
                     zero_ref, sem, zsem):
    tm = h_ref.shape[0]
    step = pl.program_id(0)

    @pl.when(step == 0)
    def _():
        zero_ref[...] = jnp.zeros_like(zero_ref)

        def tile_copy(t):
            return pltpu.make_async_copy(zero_ref, xs_ref.at[pl.ds(t * tm, tm), :], zsem)

        def fill_tile(t, carry):
            tile_copy(t).start()
            return carry

        def drain_tile(t, carry):
            tile_copy(t).wait()
            return carry

        n_tiles = xs_ref.shape[0] // tm
        lax.fori_loop(free_tile_ref[0], n_tiles, fill_tile, 0)
        lax.fori_loop(free_tile_ref[0], n_tiles, drain_tile, 0)
        for e in range(N_EXPERTS):
            start, count = pad_start_ref[e], pad_count_ref[e]
            head = (-start) & (SUBLANES - 1)
            chunks = [(i < head, _row_copy(zero_ref, 0, xs_ref, start + i, zsem))
                      for i in range(SUBLANES - 1)]
            body_start, body = start + head, count - head
            for bit in reversed(range(SUBLANES.bit_length() - 1, (MOE_TM - 1).bit_length())):
                size = 1 << bit
                assert size <= tm
                offset = pl.multiple_of(body_start + (body - (body & (2 * size - 1))), SUBLANES)
                copy = pltpu.make_async_copy(zero_ref.at[pl.ds(0, size), :],
                                             xs_ref.at[pl.ds(offset, size), :], zsem)
                chunks.append(((body & size) != 0, copy))
            for present, copy in chunks:
                pl.when(present)(copy.start)
            for present, copy in chunks:
                pl.when(present)(copy.wait)

    base = step * tm

    def send(r, carry):
        for s in range(TOP_K):
            _row_copy(h_ref, r, xs_ref, pos_ref[TOP_K * (base + r) + s], sem).start()
        return carry

    def done(r, carry):
        for s in range(TOP_K):
            _row_copy(h_ref, r, xs_ref, pos_ref[TOP_K * (base + r) + s], sem).wait()
        return carry

    lax.fori_loop(0, tm, send, 0, unroll=DMA_UNROLL)
    lax.fori_loop(0, tm, done, 0, unroll=DMA_UNROLL)


def _dispatch(h2, pos, pad_start, pad_count, free_tile, n_rows, *, tm):
    n_tok, d = h2.shape
    return pl.pallas_call(
        _dispatch_kernel,
        grid_spec=pltpu.PrefetchScalarGridSpec(
            num_scalar_prefetch=4,
            grid=(n_tok // tm,),
            in_specs=[pl.BlockSpec((tm, d), lambda i, *_: (i, 0))],
            out_specs=pl.BlockSpec(memory_space=pl.ANY),
            scratch_shapes=[pltpu.VMEM((tm, d), F32), pltpu.SemaphoreType.DMA(()),
                            pltpu.SemaphoreType.DMA(())],
        ),
        out_shape=jax.ShapeDtypeStruct((n_rows, d), F32),
        compiler_params=_cparams(("arbitrary",), has_side_effects=True),
        name="dispatch",
    )(pos, pad_start, pad_count, free_tile, h2)


def _gmoe_kernel(tile_expert_ref, n_used_ref, xs_ref, wg_ref, wu_ref, wd_ref, ys_ref, h_ref, act_ref):
    t = pl.program_id(0)
    f = pl.program_id(1)
    last = pl.num_programs(1) - 1
    active = t < n_used_ref[0]

    @pl.when(f == 0)
    def _():
        ys_ref[...] = jnp.zeros_like(ys_ref)

    @pl.when(active & (f == 0))
    def _():
        h_ref[...] = xs_ref[...].astype(BF16)
        _swiglu_up(h_ref, wg_ref, wu_ref, act_ref, 0)

    @pl.when(active & (f > 0) & (f < last))
    def _():
        _swiglu_down(act_ref, (f - 1) % 2, wd_ref, ys_ref)
        _swiglu_up(h_ref, wg_ref, wu_ref, act_ref, f % 2)

    @pl.when(active & (f == last))
    def _():
        _swiglu_down(act_ref, (f - 1) % 2, wd_ref, ys_ref)


def _gmoe(xs, tile_expert, n_used, w_gate, w_up, w_down, *, fi, tm, tf):
    n_rows, d = xs.shape
    d_ff = w_gate.shape[-1]
    nf = d_ff // tf

    def x_map(t, f, te, nu):
        return (jnp.maximum(jnp.minimum(t, nu[0] - 1), 0), 0)

    def up_idx(t, f, nu):
        return jnp.where(t < nu[0], jnp.minimum(f, nf - 1), nf - 1)

    def down_idx(t, f, nu):
        return jnp.where(t < nu[0], jnp.maximum(f - 1, 0), nf - 1)

    return pl.pallas_call(
        _gmoe_kernel,
        grid_spec=pltpu.PrefetchScalarGridSpec(
            num_scalar_prefetch=2,
            grid=(n_rows // tm, nf + 1),
            in_specs=[
                pl.BlockSpec((tm, d), x_map),
                pl.BlockSpec((None, None, d, tf), lambda t, f, te, nu: (fi, te[t], 0, up_idx(t, f, nu))),
                pl.BlockSpec((None, None, d, tf), lambda t, f, te, nu: (fi, te[t], 0, up_idx(t, f, nu))),
                pl.BlockSpec((None, None, tf, d), lambda t, f, te, nu: (fi, te[t], down_idx(t, f, nu), 0)),
            ],
            out_specs=pl.BlockSpec((tm, d), lambda t, f, te, nu: (t, 0)),
            scratch_shapes=[pltpu.VMEM((tm, d), BF16), pltpu.VMEM((2, tm, tf), BF16)],
        ),
        out_shape=jax.ShapeDtypeStruct((n_rows, d), F32),
        compiler_params=_cparams(("arbitrary", "arbitrary")),
        name="grouped_moe",
    )(tile_expert, n_used, xs, w_gate, w_up, w_down)


def _combine_kernel(pos_ref, x_ref, meta_ref, mod_ref, ys_ref, o_ref, buf_ref, sem):
    tm = x_ref.shape[0]
    base = (pl.program_id(0) * pl.num_programs(1) + pl.program_id(1)) * tm

    def fetch(r, carry):
        for s in range(TOP_K):
            _row_copy(ys_ref, pos_ref[TOP_K * (base + r) + s], buf_ref.at[s], r, sem).start()
        return carry

    def done(r, carry):
        for s in range(TOP_K):
            _row_copy(ys_ref, pos_ref[TOP_K * (base + r) + s], buf_ref.at[s], r, sem).wait()
        return carry

    lax.fori_loop(0, tm, fetch, 0, unroll=DMA_UNROLL)
    lax.fori_loop(0, tm, done, 0, unroll=DMA_UNROLL)
    meta = meta_ref[...]
    y = meta[:, META_G:META_G + 1] * buf_ref[0] + meta[:, META_G + 1:META_G + 2] * buf_ref[1]
    o_ref[...] = x_ref[...] + _mod_chunks(mod_ref, False)[5] * y


def _combine(x, meta, mods, ys, pos, *, l, tm):
    bsz, n_tok, d = x.shape
    tok = lambda b, i, *_: (b, i, 0)
    return pl.pallas_call(
        _combine_kernel,
        grid_spec=pltpu.PrefetchScalarGridSpec(
            num_scalar_prefetch=1,
            grid=(bsz, n_tok // tm),
            in_specs=[
                pl.BlockSpec((None, tm, d), tok),
                pl.BlockSpec((None, tm, META_LANES), tok),
                _mod_spec(l, d),
                pl.BlockSpec(memory_space=pl.ANY),
            ],
            out_specs=pl.BlockSpec((None, tm, d), tok),
            scratch_shapes=[pltpu.VMEM((TOP_K, tm, d), F32), pltpu.SemaphoreType.DMA(())],
        ),
        out_shape=jax.ShapeDtypeStruct(x.shape, F32),
        compiler_params=_cparams(("arbitrary", "arbitrary")),
        name="combine",
    )(pos, x, meta, mods, ys)


def _moe(x, mods, g2, w_router, w_gate, w_up, w_down, *, l, fi):
    bsz, n_tok, d = x.shape
    tm = MOE_TM
    n_all = bsz * n_tok
    n_tiles = TOP_K * n_all // tm + N_EXPERTS
    wr = jnp.pad(w_router, ((0, 0), (0, 0), (0, META_LANES - N_EXPERTS)))
    h2, meta, counts = _route(x, mods, g2, wr, l=l, fi=fi, tm=ROUTE_TM)

    meta2 = meta.reshape(n_all, META_LANES)
    expert = meta2[:, META_E:META_E + TOP_K].astype(I32)
    rank = meta2[:, META_RANK:META_RANK + TOP_K].astype(I32)
    cnt = counts[0, :N_EXPERTS].astype(I32)
    tiles = (cnt + tm - 1) // tm
    tile_end = jnp.cumsum(tiles)
    row_start = (tile_end - tiles) * tm
    expert_ids = jnp.arange(N_EXPERTS, dtype=I32)
    pos = (rank + jnp.sum(jnp.where(expert[..., None] == expert_ids, row_start, 0), axis=-1)).reshape(-1)
    n_used = tile_end[-1:]
    tile_ids = jnp.arange(n_tiles, dtype=I32)
    tile_expert = jnp.sum(tile_ids[:, None] >= tile_end[None, :], axis=1).astype(I32)
    last_expert = jnp.max(jnp.where(tiles > 0, expert_ids, 0))
    tile_expert = jnp.minimum(tile_expert, last_expert)
    pad_start = row_start + cnt
    pad_count = tiles * tm - cnt

    free_tile = n_used * (tm // ROUTE_TM)
    xs = _dispatch(h2.reshape(n_all, d), pos, pad_start, pad_count, free_tile, n_tiles * tm, tm=ROUTE_TM)
    ys = _gmoe(xs, tile_expert, n_used, w_gate, w_up, w_down, fi=fi, tm=tm, tf=512)
    return _combine(x, meta, mods, ys, pos, l=l, tm=ROUTE_TM)


N_ROW_OFF = 2 * NA_ROWS - 1
N_COL_OFF = 2 * NA_COLS - 1


def _bias_cells(rows):
    n_blocks = rows // ROW_BLOCK
    row_off = np.zeros((3, ROW_BLOCK, KEY_ROWS), np.int64)
    row_in = np.zeros((3, ROW_BLOCK, KEY_ROWS), bool)
    for kind, blk in enumerate((0, 1, n_blocks - 1)):
        win = int(np.clip(blk * ROW_BLOCK - NA_ROWS // 2, 0, rows - KEY_ROWS))
        r = blk * ROW_BLOCK + np.arange(ROW_BLOCK)[:, None]
        kr = win + np.arange(KEY_ROWS)[None, :]
        first = np.clip(r - NA_ROWS // 2, 0, rows - NA_ROWS)
        row_in[kind] = (kr >= first) & (kr < first + NA_ROWS)
        row_off[kind] = np.clip(kr - r + NA_ROWS - 1, 0, N_ROW_OFF - 1)
    return row_off, row_in


def _bias_kernel(rpb_ref, o_ref, toep_ref, *, rows):
    l, h, kind = pl.program_id(0), pl.program_id(1), pl.program_id(2)
    shape = (GRID_W, 2 * GRID_W)
    lane = lax.broadcasted_iota(I32, shape, 1)

    @pl.when(kind == 0)
    def _():
        qc = lax.broadcasted_iota(I32, shape, 0)
        kc = lane % GRID_W
        start = jnp.clip(qc - NA_COLS // 2, 0, GRID_W - NA_COLS)
        col_in = (kc >= start) & (kc < start + NA_COLS)
        col_off = jnp.clip(kc - qc, -(NA_COLS - 1), NA_COLS - 1) + (NA_COLS - 1)
        base = (l * C_HEADS + h) * (N_ROW_OFF * N_COL_OFF)
        for dr in range(N_ROW_OFF):
            t = jnp.full(shape, NEG_INF, F32)
            for dc in range(N_COL_OFF):
                t = jnp.where(col_off == dc, rpb_ref[base + dr * N_COL_OFF + dc], t)
            toep_ref[dr] = jnp.where(col_in, t, NEG_INF)

    row_off, row_in = _bias_cells(rows)
    masked = jnp.full(shape, NEG_INF, F32)
    for k in range(3):
        @pl.when(kind == k)
        def _():
            for i in range(ROW_BLOCK):
                for jp in range(KEY_ROWS // 2):
                    halves = [toep_ref[int(row_off[k, i, jj])] if row_in[k, i, jj] else masked
                              for jj in (2 * jp, 2 * jp + 1)]
                    o_ref[i * GRID_W:(i + 1) * GRID_W, jp * 2 * GRID_W:(jp + 1) * 2 * GRID_W] = (
                        jnp.where(lane < GRID_W, halves[0], halves[1]))


def _attention_bias(rpb, rows):
    depth = rpb.shape[0]
    assert KEY_ROWS % 2 == 0
    return pl.pallas_call(
        functools.partial(_bias_kernel, rows=rows),
        grid=(depth, C_HEADS, 3),
        in_specs=[pl.BlockSpec(memory_space=pltpu.SMEM)],
        out_specs=pl.BlockSpec((None, None, None, ROW_BLOCK * GRID_W, KEY_ROWS * GRID_W),
                               lambda l, h, k: (l, k, h, 0, 0)),
        out_shape=jax.ShapeDtypeStruct(
            (depth, 3, C_HEADS, ROW_BLOCK * GRID_W, KEY_ROWS * GRID_W), F32),
        scratch_shapes=[pltpu.VMEM((N_ROW_OFF, GRID_W, 2 * GRID_W), F32)],
        compiler_params=_cparams(("arbitrary", "arbitrary", "arbitrary")),
        name="attention_bias",
    )(rpb.astype(F32).reshape(-1))


def _block_diag(blocks):
    g, m, n = blocks.shape[-3:]
    eye = jnp.eye(g, dtype=blocks.dtype)
    out = blocks[..., :, :, None, :] * eye[:, None, :, None]
    return out.reshape(blocks.shape[:-3] + (g * m, g * n))


def kernel(x, c, ctx, c_ctx, w_mod, b_mod, norm1_g, norm2_g, w_in, w_out, a_vn_g, a_vn_b, a_ws, a_bs,
           b_wpool, b_scale, c_qn_g, c_kn_g, c_rpb, ffn_w_gate, ffn_w_up, ffn_w_down,
           moe_w_router, moe_w_gate, moe_w_up, moe_w_down):
    bsz, seq_len, d = x.shape
    depth = w_mod.shape[0]
    rows = seq_len // GRID_W
    assert bsz <= CTX_MOD_ROW

    cond = jnp.concatenate([c, jnp.zeros((CTX_MOD_ROW - bsz, d), F32), c_ctx[None, :],
                            jnp.zeros((MOD_ROWS - CTX_MOD_ROW - 1, d), F32)], axis=0)
    mods = _modulation(cond, w_mod, b_mod)
    hsum = jnp.asarray(np.kron(np.eye(C_HEADS), np.full((HEAD_DIM, HEAD_DIM), 1.0 / HEAD_DIM)), BF16)

    g1 = norm1_g.reshape(depth, 1, d)
    g2 = norm2_g.reshape(depth, 1, d)
    w_in_b = w_in.astype(BF16)
    qg = (jnp.tile(c_qn_g, (1, C_HEADS)) * ATTN_SCALE).reshape(depth, 1, C_WIDTH)
    kg = jnp.tile(c_kn_g, (1, C_HEADS)).reshape(depth, 1, C_WIDTH)
    mix_w = (a_vn_g.reshape(depth, 1, A_WIDTH), a_vn_b.reshape(depth, 1, A_WIDTH),
             a_ws.astype(BF16),
             jnp.repeat(jnp.swapaxes(a_bs, 1, 2), HEAD_DIM, axis=2),
             _block_diag(b_wpool).astype(BF16),
             b_scale.reshape(depth, 1, B_WIDTH),
             w_out.astype(BF16))
    bias = _attention_bias(c_rpb, rows)

    xc = ctx
    for l in range(depth):
        last = l == depth - 1
        fi = l // 2
        uv_c, z_c, q_c, k_c, v_c = _in_proj(xc, mods, g1, w_in_b, hsum, qg, kg,
                                            l=l, is_ctx=True, tm=xc.shape[1])
        uv, z, q, k, v = _in_proj(x, mods, g1, w_in_b, hsum, qg, kg, l=l, is_ctx=False, tm=512)
        x = _mix(x, uv, z, q, k, v, k_c, v_c, bias, mods, *mix_w, l=l)

        ffn_w = (ffn_w_gate, ffn_w_up, ffn_w_down)
        if l % 2 == 0:
            x = _ffn(x, mods, g2, *ffn_w, l=l, fi=fi, is_ctx=False, tm=1024, tf=256)
        else:
            x = _moe(x, mods, g2, moe_w_router, moe_w_gate, moe_w_up, moe_w_down, l=l, fi=fi)

        if not last:
            xc = _mix_ctx(xc, uv_c, z_c, q_c, k_c, v_c, mods, *mix_w, l=l)
            n_ctx = xc.shape[1]
            if l % 2 == 0:
                xc = _ffn(xc.reshape(1, bsz * n_ctx, d), mods, g2, *ffn_w, l=l, fi=fi,
                          is_ctx=True, tm=bsz * n_ctx, tf=256).reshape(bsz, n_ctx, d)
            else:
                raise NotImplementedError("a context-stream MoE layer only occurs for depth > 2")
    return x
```

```python
import functools

import numpy as np
import jax
import jax.numpy as jnp
from jax import lax
from jax.experimental import pallas as pl
from jax.experimental.pallas import tpu as pltpu

F32 = jnp.float32
BF16 = jnp.bfloat16
I32 = jnp.int32

D_MODEL = 1024
GRID_W = 64
HEAD_DIM = 64
A_WIDTH = 256
A_GROUPS = 4
CHUNK = 128
B_WIDTH = 256
POOL_WINDOWS = (2, 4, 8, 16)
POOL_HALO = 8
C_WIDTH = 512
C_HEADS = 8
NA_ROWS = 8
NA_COLS = 16
ATTN_SCALE = HEAD_DIM ** -0.5
IN_COLS = 2 * A_WIDTH + B_WIDTH + 3 * C_WIDTH
N_EXPERTS = 8
TOP_K = 2
EPS = 1e-6
NEG_INF = -1e30

ROW_BLOCK = 4
KEY_ROWS = ROW_BLOCK + NA_ROWS
assert ROW_BLOCK >= NA_ROWS // 2
MOD_ROWS = 8
CTX_MOD_ROW = 4
META_LANES = 128
META_E, META_G, META_RANK = 0, 2, 4
MOE_TM = 512
ROUTE_TM = 512
DMA_UNROLL = 8
SUBLANES = 8

VMEM_LIMIT = 60 * 1024 * 1024


def _cparams(sem, **kw):
    return pltpu.CompilerParams(dimension_semantics=sem, vmem_limit_bytes=VMEM_LIMIT, **kw)


def _dot(a, b):
    return jnp.dot(a, b, preferred_element_type=F32)


def _silu(t):
    return t / (1.0 + jnp.exp(-t))


def _gelu(t):
    return 0.5 * t * (1.0 + lax.erf(t * np.float32(np.sqrt(0.5))))


def _norm_mod(x, g, scale, shift):
    y = x * lax.rsqrt(jnp.mean(x * x, axis=-1, keepdims=True) + EPS) * g
    return y * (1.0 + scale) + shift


def _mod_chunks(mod_ref, is_ctx):
    row = CTX_MOD_ROW if is_ctx else pl.program_id(0)
    m = mod_ref[pl.ds(row, 1), :]
    d = m.shape[1] // 6
    return [m[:, k * d:(k + 1) * d] for k in range(6)]


def _mod_spec(l, d):
    return pl.BlockSpec((None, MOD_ROWS, 6 * d), lambda *_: (l, 0, 0))


def _layer_spec(l, shape):
    zeros = (0,) * len(shape)
    return pl.BlockSpec((None,) + tuple(shape), lambda *_: (l,) + zeros)


def _mod_kernel(cond_ref, w_ref, b_ref, o_ref):
    s = _silu(cond_ref[...])
    o_ref[...] = _dot(s.astype(BF16), w_ref[...].astype(BF16)) + b_ref[...]


def _modulation(cond, w_mod, b_mod):
    depth, d, cols = w_mod.shape
    tn = 1536
    return pl.pallas_call(
        _mod_kernel,
        grid=(depth, cols // tn),
        in_specs=[
            pl.BlockSpec((MOD_ROWS, d), lambda l, j: (0, 0)),
            pl.BlockSpec((None, d, tn), lambda l, j: (l, 0, j)),
            pl.BlockSpec((None, 1, tn), lambda l, j: (l, 0, j)),
        ],
        out_specs=pl.BlockSpec((None, MOD_ROWS, tn), lambda l, j: (l, 0, j)),
        out_shape=jax.ShapeDtypeStruct((depth, MOD_ROWS, cols), F32),
        compiler_params=_cparams(("arbitrary", "arbitrary")),
        name="modulation",
    )(cond, w_mod, b_mod.reshape(depth, 1, cols))


def _head_rms(t, hsum):
    ms = _dot((t * t).astype(BF16), hsum)
    return t * lax.rsqrt(ms + EPS)


def _in_proj_kernel(x_ref, mod_ref, g_ref, w_ref, hs_ref, qg_ref, kg_ref,
                    uv_ref, z_ref, q_ref, kt_ref, v_ref, *, is_ctx):
    shift, scale = _mod_chunks(mod_ref, is_ctx)[:2]
    h = _norm_mod(x_ref[...], g_ref[...], scale, shift)
    p = _dot(h.astype(BF16), w_ref[...])
    o = 2 * A_WIDTH
    uv_ref[...] = p[:, :o].astype(BF16)
    z_ref[...] = p[:, o:o + B_WIDTH]
    o += B_WIDTH
    hs = hs_ref[...]
    q_ref[...] = (_head_rms(p[:, o:o + C_WIDTH], hs) * qg_ref[...]).astype(BF16)
    o += C_WIDTH
    kt_ref[...] = (_head_rms(p[:, o:o + C_WIDTH], hs) * kg_ref[...]).T.astype(BF16)
    o += C_WIDTH
    v_ref[...] = p[:, o:o + C_WIDTH].astype(BF16)


def _in_proj(x, mods, g1, w_in, hsum, qg, kg, *, l, is_ctx, tm):
    bsz, n_tok, d = x.shape
    const = lambda b, i: (0, 0)
    tok = lambda b, i: (b, i, 0)

    def out(width, dtype):
        return (pl.BlockSpec((None, tm, width), tok),
                jax.ShapeDtypeStruct((bsz, n_tok, width), dtype))

    keys_t = (pl.BlockSpec((None, C_WIDTH, tm), lambda b, i: (b, 0, i)),
              jax.ShapeDtypeStruct((bsz, C_WIDTH, n_tok), BF16))
    outs = [out(2 * A_WIDTH, BF16), out(B_WIDTH, F32), out(C_WIDTH, BF16),
            keys_t, out(C_WIDTH, BF16)]
    return pl.pallas_call(
        functools.partial(_in_proj_kernel, is_ctx=is_ctx),
        grid=(bsz, n_tok // tm),
        in_specs=[
            pl.BlockSpec((None, tm, d), tok),
            _mod_spec(l, d),
            _layer_spec(l, (1, d)),
            _layer_spec(l, (d, IN_COLS)),
            pl.BlockSpec((C_WIDTH, C_WIDTH), const),
            _layer_spec(l, (1, C_WIDTH)),
            _layer_spec(l, (1, C_WIDTH)),
        ],
        out_specs=[o[0] for o in outs],
        out_shape=[o[1] for o in outs],
        compiler_params=_cparams(("arbitrary", "arbitrary")),
        name="in_proj_ctx" if is_ctx else "in_proj",
    )(x, mods, g1, w_in, hsum, qg, kg)


def _mixer_a(uv, vn_g, vn_b, ws_ref, bs_full):
    n_tok = uv.shape[0]
    u = _gelu(uv[:, :A_WIDTH])
    v = _gelu(uv[:, A_WIDTH:])
    vc = v - jnp.mean(v, axis=-1, keepdims=True)
    var = jnp.mean(vc * vc, axis=-1, keepdims=True)
    v = (vc * lax.rsqrt(var + EPS) * vn_g + vn_b).astype(BF16)
    lane_group = lax.broadcasted_iota(I32, (CHUNK, A_WIDTH), 1) // HEAD_DIM
    outs = []
    for c in range(n_tok // CHUNK):
        v_c = v[c * CHUNK:(c + 1) * CHUNK]
        mixed = bs_full
        for g in range(A_GROUPS):
            mixed = mixed + jnp.where(lane_group == g, _dot(ws_ref[g], v_c), 0.0)
        outs.append(u[c * CHUNK:(c + 1) * CHUNK] * mixed)
    return jnp.concatenate(outs, axis=0) if len(outs) > 1 else outs[0]


def _mixer_b(zh_ref, t0, n_tok, seq_len, wpool_bd, b_scale):
    def sh(d):
        return zh_ref[POOL_HALO + d:POOL_HALO + d + n_tok, :]

    z = sh(0)
    s = sh(-1) + z
    sums = [s]
    for win in POOL_WINDOWS[1:]:
        half = win // 2
        for d in range(-half, -half // 2):
            s = s + sh(d)
        for d in range(half // 2, half):
            s = s + sh(d)
        sums.append(s)
    lane_group = lax.broadcasted_iota(I32, (n_tok, B_WIDTH), 1) // (B_WIDTH // len(POOL_WINDOWS))
    pos = t0 + lax.broadcasted_iota(I32, (n_tok, B_WIDTH), 0)
    half = jnp.left_shift(1, lane_group)
    cnt = jnp.minimum(pos + half, seq_len) - jnp.maximum(pos - half, 0)
    total = sums[0]
    for g in range(1, len(POOL_WINDOWS)):
        total = jnp.where(lane_group == g, sums[g], total)
    y = total / cnt.astype(F32) - z
    return _dot(y.astype(BF16), wpool_bd) * b_scale


def _fill_halo(zh_ref, z_ref, t0, n_tok, seq_len):
    zh_ref[POOL_HALO:POOL_HALO + n_tok, :] = z_ref[pl.ds(t0, n_tok), :]
    lo = jnp.maximum(t0 - POOL_HALO, 0)
    hi = jnp.minimum(t0 + n_tok, seq_len - POOL_HALO)
    lo = pl.multiple_of(lo, POOL_HALO)
    hi = pl.multiple_of(hi, POOL_HALO)
    before = z_ref[pl.ds(lo, POOL_HALO), :]
    after = z_ref[pl.ds(hi, POOL_HALO), :]
    zh_ref[0:POOL_HALO, :] = jnp.where(t0 > 0, before, 0.0)
    zh_ref[POOL_HALO + n_tok:, :] = jnp.where(t0 + n_tok < seq_len, after, 0.0)


def _attend(q_h, kt_h, v_h, bias_h, kct_h, vc_h):
    s = _dot(q_h, kt_h)
    if bias_h is not None:
        s = s + bias_h
    m = jnp.max(s, axis=-1, keepdims=True)
    if kct_h is not None:
        sc = _dot(q_h, kct_h)
        m = jnp.maximum(m, jnp.max(sc, axis=-1, keepdims=True))
        pc = jnp.exp(sc - m)
    p = jnp.exp(s - m)
    den = jnp.sum(p, axis=-1, keepdims=True)
    o = _dot(p.astype(BF16), v_h)
    if kct_h is not None:
        den = den + jnp.sum(pc, axis=-1, keepdims=True)
        o = o + _dot(pc.astype(BF16), vc_h)
    return o / den


def _mix_kernel(x_ref, uv_ref, z_ref, q_ref, kt_ref, v_ref, kct_ref, vc_ref, bias_ref, mod_ref,
                vn_g_ref, vn_b_ref, ws_ref, bs_ref, wpool_ref, bscale_ref, wout_ref,
                o_ref, zh_ref, mix_ref, *, seq_len):
    gate1 = _mod_chunks(mod_ref, False)[2]
    j = pl.program_id(1)
    n_tok = ROW_BLOCK * GRID_W
    t0 = pl.multiple_of(j * n_tok, n_tok)
    rows = seq_len // GRID_W

    a = _mixer_a(uv_ref[...].astype(F32), vn_g_ref[...], vn_b_ref[...], ws_ref, bs_ref[...])
    mix_ref[:, :A_WIDTH] = a.astype(BF16)

    _fill_halo(zh_ref, z_ref, t0, n_tok, seq_len)
    bmix = _mixer_b(zh_ref, t0, n_tok, seq_len, wpool_ref[...], bscale_ref[...])
    mix_ref[:, A_WIDTH:A_WIDTH + B_WIDTH] = bmix.astype(BF16)

    win_row = jnp.clip(j * ROW_BLOCK - NA_ROWS // 2, 0, rows - KEY_ROWS)
    koff = pl.multiple_of(win_row * GRID_W, 2 * GRID_W)
    c_off = A_WIDTH + B_WIDTH
    for h in range(C_HEADS):
        hs = slice(h * HEAD_DIM, (h + 1) * HEAD_DIM)
        o = _attend(q_ref[:, hs],
                    kt_ref[hs, pl.ds(koff, KEY_ROWS * GRID_W)],
                    v_ref[pl.ds(koff, KEY_ROWS * GRID_W), hs],
                    bias_ref[h], kct_ref[hs, :], vc_ref[:, hs])
        mix_ref[:, c_off + h * HEAD_DIM:c_off + (h + 1) * HEAD_DIM] = o.astype(BF16)

    y = _dot(mix_ref[...], wout_ref[...])
    o_ref[...] = x_ref[...] + gate1 * y


def _mix(x, uv, z, q, k, v, kc, vc, bias, mods, vn_g, vn_b, ws, bs_full, wpool_bd, b_scale, w_out, *, l):
    bsz, seq_len, d = x.shape
    n_ctx = vc.shape[1]
    n_tok = ROW_BLOCK * GRID_W
    n_blocks = seq_len // n_tok
    assert ROW_BLOCK % 2 == 0 and (NA_ROWS // 2) % 2 == 0 and (seq_len // GRID_W - KEY_ROWS) % 2 == 0
    tok = lambda b, j: (b, j, 0)
    full = lambda b, j: (b, 0, 0)
    c2 = lambda b, j: (0, 0)
    c3 = lambda b, j: (0, 0, 0)

    def bias_map(b, j):
        kind = jnp.where(j == 0, 0, jnp.where(j == n_blocks - 1, 2, 1))
        return (l, kind, 0, 0, 0)

    return pl.pallas_call(
        functools.partial(_mix_kernel, seq_len=seq_len),
        grid=(bsz, n_blocks),
        in_specs=[
            pl.BlockSpec((None, n_tok, d), tok),
            pl.BlockSpec((None, n_tok, 2 * A_WIDTH), tok),
            pl.BlockSpec((None, seq_len, B_WIDTH), full),
            pl.BlockSpec((None, n_tok, C_WIDTH), tok),
            pl.BlockSpec((None, C_WIDTH, seq_len), full),
            pl.BlockSpec((None, seq_len, C_WIDTH), full),
            pl.BlockSpec((None, C_WIDTH, n_ctx), full),
            pl.BlockSpec((None, n_ctx, C_WIDTH), full),
            pl.BlockSpec((None, None, C_HEADS, n_tok, KEY_ROWS * GRID_W), bias_map),
            _mod_spec(l, d),
            _layer_spec(l, (1, A_WIDTH)),
            _layer_spec(l, (1, A_WIDTH)),
            _layer_spec(l, (A_GROUPS, CHUNK, CHUNK)),
            _layer_spec(l, (CHUNK, A_WIDTH)),
            _layer_spec(l, (B_WIDTH, B_WIDTH)),
            _layer_spec(l, (1, B_WIDTH)),
            _layer_spec(l, (d, d)),
        ],
        out_specs=pl.BlockSpec((None, n_tok, d), tok),
        out_shape=jax.ShapeDtypeStruct(x.shape, F32),
        scratch_shapes=[
            pltpu.VMEM((n_tok + 2 * POOL_HALO, B_WIDTH), F32),
            pltpu.VMEM((n_tok, d), BF16),
        ],
        compiler_params=_cparams(("arbitrary", "arbitrary")),
        name="mix",
    )(x, uv, z, q, k, v, kc, vc, bias, mods, vn_g, vn_b, ws, bs_full, wpool_bd, b_scale, w_out)


def _mix_ctx_kernel(x_ref, uv_ref, z_ref, q_ref, kt_ref, v_ref, mod_ref,
                    vn_g_ref, vn_b_ref, ws_ref, bs_ref, wpool_ref, bscale_ref, wout_ref,
                    o_ref, zh_ref, mix_ref):
    gate1 = _mod_chunks(mod_ref, True)[2]
    n_tok = x_ref.shape[0]
    a = _mixer_a(uv_ref[...].astype(F32), vn_g_ref[...], vn_b_ref[...], ws_ref, bs_ref[...])
    mix_ref[:, :A_WIDTH] = a.astype(BF16)

    zero = jnp.zeros((POOL_HALO, B_WIDTH), F32)
    zh_ref[0:POOL_HALO, :] = zero
    zh_ref[POOL_HALO:POOL_HALO + n_tok, :] = z_ref[...]
    zh_ref[POOL_HALO + n_tok:, :] = zero
    bmix = _mixer_b(zh_ref, 0, n_tok, n_tok, wpool_ref[...], bscale_ref[...])
    mix_ref[:, A_WIDTH:A_WIDTH + B_WIDTH] = bmix.astype(BF16)

    c_off = A_WIDTH + B_WIDTH
    for h in range(C_HEADS):
        hs = slice(h * HEAD_DIM, (h + 1) * HEAD_DIM)
        o = _attend(q_ref[:, hs], kt_ref[hs, :], v_ref[:, hs], None, None, None)
        mix_ref[:, c_off + h * HEAD_DIM:c_off + (h + 1) * HEAD_DIM] = o.astype(BF16)

    y = _dot(mix_ref[...], wout_ref[...])
    o_ref[...] = x_ref[...] + gate1 * y


def _mix_ctx(x, uv, z, q, k, v, mods, vn_g, vn_b, ws, bs_full, wpool_bd, b_scale, w_out, *, l):
    bsz, n_tok, d = x.shape
    full = lambda b: (b, 0, 0)
    c2 = lambda b: (0, 0)
    c3 = lambda b: (0, 0, 0)
    return pl.pallas_call(
        _mix_ctx_kernel,
        grid=(bsz,),
        in_specs=[
            pl.BlockSpec((None, n_tok, d), full),
            pl.BlockSpec((None, n_tok, 2 * A_WIDTH), full),
            pl.BlockSpec((None, n_tok, B_WIDTH), full),
            pl.BlockSpec((None, n_tok, C_WIDTH), full),
            pl.BlockSpec((None, C_WIDTH, n_tok), full),
            pl.BlockSpec((None, n_tok, C_WIDTH), full),
            _mod_spec(l, d),
            _layer_spec(l, (1, A_WIDTH)),
            _layer_spec(l, (1, A_WIDTH)),
            _layer_spec(l, (A_GROUPS, CHUNK, CHUNK)),
            _layer_spec(l, (CHUNK, A_WIDTH)),
            _layer_spec(l, (B_WIDTH, B_WIDTH)),
            _layer_spec(l, (1, B_WIDTH)),
            _layer_spec(l, (d, d)),
        ],
        out_specs=pl.BlockSpec((None, n_tok, d), full),
        out_shape=jax.ShapeDtypeStruct(x.shape, F32),
        scratch_shapes=[
            pltpu.VMEM((n_tok + 2 * POOL_HALO, B_WIDTH), F32),
            pltpu.VMEM((n_tok, d), BF16),
        ],
        compiler_params=_cparams(("arbitrary",)),
        name="mix_ctx",
    )(x, uv, z, q, k, v, mods, vn_g, vn_b, ws, bs_full, wpool_bd, b_scale, w_out)


def _cache_weights(fresh, f, last, wg_ref, wu_ref, wd_ref, cg_ref, cu_ref, cd_ref):
    @pl.when(fresh & (f < last))
    def _():
        cg_ref[f] = wg_ref[...].astype(BF16)
        cu_ref[f] = wu_ref[...].astype(BF16)

    @pl.when(fresh & (f > 0))
    def _():
        cd_ref[f - 1] = wd_ref[...].astype(BF16)


def _swiglu_step(run, f, last, h_ref, acc_ref, act_ref, cg_ref, cu_ref, cd_ref):
    def up():
        h = h_ref[...]
        act = _silu(_dot(h, cg_ref[f])) * _dot(h, cu_ref[f])
        act_ref[f % 2] = act.astype(BF16)

    def down():
        acc_ref[...] += _dot(act_ref[(f - 1) % 2], cd_ref[f - 1])

    pl.when(run & (f == 0))(up)

    @pl.when(run & (f > 0) & (f < last))
    def _():
        down()
        up()

    pl.when(run & (f == last))(down)


def _ffn_kernel(x_ref, mod_ref, g_ref, wg_ref, wu_ref, wd_ref, o_ref, h_ref, acc_ref, act_ref,
                cg_ref, cu_ref, cd_ref, *, is_ctx):
    f = pl.program_id(2)
    last = pl.num_programs(2) - 1
    _, _, _, shift, scale, gate = _mod_chunks(mod_ref, is_ctx)
    fresh = (pl.program_id(0) == 0) & (pl.program_id(1) == 0)
    _cache_weights(fresh, f, last, wg_ref, wu_ref, wd_ref, cg_ref, cu_ref, cd_ref)

    @pl.when(f == 0)
    def _():
        h = _norm_mod(x_ref[...], g_ref[...], scale, shift)
        h_ref[...] = h.astype(BF16)
        acc_ref[...] = jnp.zeros_like(acc_ref)

    _swiglu_step(True, f, last, h_ref, acc_ref, act_ref, cg_ref, cu_ref, cd_ref)

    @pl.when(f == last)
    def _():
        o_ref[...] = x_ref[...] + gate * acc_ref[...]


def _ffn(x, mods, g2, w_gate, w_up, w_down, *, l, fi, is_ctx, tm, tf):
    bsz, n_tok, d = x.shape
    d_ff = w_gate.shape[-1]
    nf = d_ff // tf
    tok = lambda b, i, f: (b, i, 0)

    def up_idx(b, i, f):
        return jnp.where((b == 0) & (i == 0), jnp.minimum(f, nf - 1), nf - 1)

    def down_idx(b, i, f):
        return jnp.where((b == 0) & (i == 0), jnp.maximum(f - 1, 0), nf - 1)

    return pl.pallas_call(
        functools.partial(_ffn_kernel, is_ctx=is_ctx),
        grid=(bsz, n_tok // tm, nf + 1),
        in_specs=[
            pl.BlockSpec((None, tm, d), tok),
            _mod_spec(l, d),
            _layer_spec(l, (1, d)),
            pl.BlockSpec((None, d, tf), lambda b, i, f: (fi, 0, up_idx(b, i, f))),
            pl.BlockSpec((None, d, tf), lambda b, i, f: (fi, 0, up_idx(b, i, f))),
            pl.BlockSpec((None, tf, d), lambda b, i, f: (fi, down_idx(b, i, f), 0)),
        ],
        out_specs=pl.BlockSpec((None, tm, d), tok),
        out_shape=jax.ShapeDtypeStruct(x.shape, F32),
        scratch_shapes=[pltpu.VMEM((tm, d), BF16), pltpu.VMEM((tm, d), F32),
                        pltpu.VMEM((2, tm, tf), BF16), pltpu.VMEM((nf, d, tf), BF16),
                        pltpu.VMEM((nf, d, tf), BF16), pltpu.VMEM((nf, tf, d), BF16)],
        compiler_params=_cparams(("arbitrary", "arbitrary", "arbitrary")),
        name="ffn_ctx" if is_ctx else "ffn",
    )(x, mods, g2, w_gate, w_up, w_down)


def _route_kernel(x_ref, mod_ref, g_ref, wr_ref, h_ref, meta_ref, cnt_ref, run_ref):
    @pl.when((pl.program_id(0) == 0) & (pl.program_id(1) == 0))
    def _():
        run_ref[...] = jnp.zeros_like(run_ref)

    _, _, _, shift, scale, _ = _mod_chunks(mod_ref, False)
    h = _norm_mod(x_ref[...], g_ref[...], scale, shift)
    h_ref[...] = h
    logits = jnp.dot(h, wr_ref[...], preferred_element_type=F32, precision=lax.Precision.HIGHEST)
    tm = logits.shape[0]
    lane = lax.broadcasted_iota(I32, logits.shape, 1)
    logits = jnp.where(lane < N_EXPERTS, logits, -jnp.inf)
    m1 = jnp.max(logits, axis=-1, keepdims=True)
    i1 = jnp.min(jnp.where(logits == m1, lane, META_LANES), axis=-1, keepdims=True)
    rest = jnp.where(lane == i1, -jnp.inf, logits)
    m2 = jnp.max(rest, axis=-1, keepdims=True)
    i2 = jnp.min(jnp.where(rest == m2, lane, META_LANES), axis=-1, keepdims=True)
    e2 = jnp.exp(m2 - m1)
    den = 1.0 + e2

    pick1 = lane == i1
    pick2 = lane == i2
    onehot = jnp.where(pick1, 1.0, 0.0) + jnp.where(pick2, 1.0, 0.0)
    earlier = (lax.broadcasted_iota(I32, (tm, tm), 0) > lax.broadcasted_iota(I32, (tm, tm), 1))
    before = run_ref[...] + _dot(jnp.where(earlier, 1.0, 0.0).astype(BF16), onehot.astype(BF16))
    r1 = jnp.sum(jnp.where(pick1, before, 0.0), axis=-1, keepdims=True)
    r2 = jnp.sum(jnp.where(pick2, before, 0.0), axis=-1, keepdims=True)
    run_ref[...] += jnp.sum(onehot, axis=0, keepdims=True)
    cnt_ref[...] = run_ref[...]

    meta = jnp.zeros(logits.shape, F32)
    for col, val in ((META_E, i1.astype(F32)), (META_E + 1, i2.astype(F32)),
                     (META_G, 1.0 / den), (META_G + 1, e2 / den),
                     (META_RANK, r1), (META_RANK + 1, r2)):
        meta = jnp.where(lane == col, val, meta)
    meta_ref[...] = meta


def _route(x, mods, g2, w_router_pad, *, l, fi, tm):
    bsz, n_tok, d = x.shape
    tok = lambda b, i: (b, i, 0)
    return pl.pallas_call(
        _route_kernel,
        grid=(bsz, n_tok // tm),
        in_specs=[
            pl.BlockSpec((None, tm, d), tok),
            _mod_spec(l, d),
            _layer_spec(l, (1, d)),
            _layer_spec(fi, (d, META_LANES)),
        ],
        out_specs=[
            pl.BlockSpec((None, tm, d), tok),
            pl.BlockSpec((None, tm, META_LANES), tok),
            pl.BlockSpec((1, META_LANES), lambda b, i: (0, 0)),
        ],
        out_shape=[
            jax.ShapeDtypeStruct((bsz, n_tok, d), F32),
            jax.ShapeDtypeStruct((bsz, n_tok, META_LANES), F32),
            jax.ShapeDtypeStruct((1, META_LANES), F32),
        ],
        scratch_shapes=[pltpu.VMEM((1, META_LANES), F32)],
        compiler_params=_cparams(("arbitrary", "arbitrary")),
        name="route",
    )(x, mods, g2, w_router_pad)


def _row_copy(src_ref, src_row, dst_ref, dst_row, sem):
    return pltpu.make_async_copy(src_ref.at[pl.ds(src_row, 1), :], dst_ref.at[pl.ds(dst_row, 1), :], sem)


def _dispatch_kernel(pos_ref, pad_start_ref, pad_count_ref, free_tile_ref, h_ref, xs_ref,
                     zero_ref, sem, zsem):
    tm = h_ref.shape[0]
    step = pl.program_id(0)

    @pl.when(step == 0)
    def _():
        zero_ref[...] = jnp.zeros_like(zero_ref)

        def tile_copy(t):
            return pltpu.make_async_copy(zero_ref, xs_ref.at[pl.ds(t * tm, tm), :], zsem)

        def fill_tile(t, carry):
            tile_copy(t).start()
            return carry

        def drain_tile(t, carry):
            tile_copy(t).wait()
            return carry

        n_tiles = xs_ref.shape[0] // tm
        lax.fori_loop(free_tile_ref[0], n_tiles, fill_tile, 0)
        lax.fori_loop(free_tile_ref[0], n_tiles, drain_tile, 0)
        for e in range(N_EXPERTS):
            start, count = pad_start_ref[e], pad_count_ref[e]
            head = (-start) & (SUBLANES - 1)
            chunks = [(i < head, _row_copy(zero_ref, 0, xs_ref, start + i, zsem))
                      for i in range(SUBLANES - 1)]
            body_start, body = start + head, count - head
            for bit in reversed(range(SUBLANES.bit_length() - 1, (MOE_TM - 1).bit_length())):
                size = 1 << bit
                assert size <= tm
                offset = pl.multiple_of(body_start + (body - (body & (2 * size - 1))), SUBLANES)
                copy = pltpu.make_async_copy(zero_ref.at[pl.ds(0, size), :],
                                             xs_ref.at[pl.ds(offset, size), :], zsem)
                chunks.append(((body & size) != 0, copy))
            for present, copy in chunks:
                pl.when(present)(copy.start)
            for present, copy in chunks:
                pl.when(present)(copy.wait)

    base = step * tm

    def send(r, carry):
        for s in range(TOP_K):
            _row_copy(h_ref, r, xs_ref, pos_ref[TOP_K * (base + r) + s], sem).start()
        return carry

    def done(r, carry):
        for s in range(TOP_K):
            _row_copy(h_ref, r, xs_ref, pos_ref[TOP_K * (base + r) + s], sem).wait()
        return carry

    lax.fori_loop(0, tm, send, 0, unroll=DMA_UNROLL)
    lax.fori_loop(0, tm, done, 0, unroll=DMA_UNROLL)


def _dispatch(h2, pos, pad_start, pad_count, free_tile, n_rows, *, tm):
    n_tok, d = h2.shape
    return pl.pallas_call(
        _dispatch_kernel,
        grid_spec=pltpu.PrefetchScalarGridSpec(
            num_scalar_prefetch=4,
            grid=(n_tok // tm,),
            in_specs=[pl.BlockSpec((tm, d), lambda i, *_: (i, 0))],
            out_specs=pl.BlockSpec(memory_space=pl.ANY),
            scratch_shapes=[pltpu.VMEM((tm, d), F32), pltpu.SemaphoreType.DMA(()),
                            pltpu.SemaphoreType.DMA(())],
        ),
        out_shape=jax.ShapeDtypeStruct((n_rows, d), F32),
        compiler_params=_cparams(("arbitrary",), has_side_effects=True),
        name="dispatch",
    )(pos, pad_start, pad_count, free_tile, h2)


def _gmoe_kernel(tile_expert_ref, tile_first_ref, n_used_ref, xs_ref, wg_ref, wu_ref, wd_ref,
                 ys_ref, h_ref, act_ref, cg_ref, cu_ref, cd_ref):
    t = pl.program_id(0)
    f = pl.program_id(1)
    last = pl.num_programs(1) - 1
    active = t < n_used_ref[0]
    fresh = active & (tile_first_ref[t] == 1)
    _cache_weights(fresh, f, last, wg_ref, wu_ref, wd_ref, cg_ref, cu_ref, cd_ref)

    @pl.when(f == 0)
    def _():
        ys_ref[...] = jnp.zeros_like(ys_ref)

    @pl.when(active & (f == 0))
    def _():
        h_ref[...] = xs_ref[...].astype(BF16)

    _swiglu_step(active, f, last, h_ref, ys_ref, act_ref, cg_ref, cu_ref, cd_ref)


def _gmoe(xs, tile_expert, tile_first, n_used, w_gate, w_up, w_down, *, fi, tm, tf):
    n_rows, d = xs.shape
    d_ff = w_gate.shape[-1]
    nf = d_ff // tf

    def x_map(t, f, te, first, nu):
        return (jnp.maximum(jnp.minimum(t, nu[0] - 1), 0), 0)

    def up_map(t, f, te, first, nu):
        stream = (t < nu[0]) & (first[t] == 1)
        return (fi, te[t], 0, jnp.where(stream, jnp.minimum(f, nf - 1), nf - 1))

    def down_map(t, f, te, first, nu):
        stream = (t < nu[0]) & (first[t] == 1)
        return (fi, te[t], jnp.where(stream, jnp.maximum(f - 1, 0), nf - 1), 0)

    return pl.pallas_call(
        _gmoe_kernel,
        grid_spec=pltpu.PrefetchScalarGridSpec(
            num_scalar_prefetch=3,
            grid=(n_rows // tm, nf + 1),
            in_specs=[
                pl.BlockSpec((tm, d), x_map),
                pl.BlockSpec((None, None, d, tf), up_map),
                pl.BlockSpec((None, None, d, tf), up_map),
                pl.BlockSpec((None, None, tf, d), down_map),
            ],
            out_specs=pl.BlockSpec((tm, d), lambda t, f, te, first, nu: (t, 0)),
            scratch_shapes=[pltpu.VMEM((tm, d), BF16), pltpu.VMEM((2, tm, tf), BF16),
                            pltpu.VMEM((nf, d, tf), BF16), pltpu.VMEM((nf, d, tf), BF16),
                            pltpu.VMEM((nf, tf, d), BF16)],
        ),
        out_shape=jax.ShapeDtypeStruct((n_rows, d), F32),
        compiler_params=_cparams(("arbitrary", "arbitrary")),
        name="grouped_moe",
    )(tile_expert, tile_first, n_used, xs, w_gate, w_up, w_down)


def _combine_kernel(pos_ref, x_ref, meta_ref, mod_ref, ys_ref, o_ref, buf_ref, sem):
    tm = x_ref.shape[0]
    base = (pl.program_id(0) * pl.num_programs(1) + pl.program_id(1)) * tm

    def fetch(r, carry):
        for s in range(TOP_K):
            _row_copy(ys_ref, pos_ref[TOP_K * (base + r) + s], buf_ref.at[s], r, sem).start()
        return carry

    def done(r, carry):
        for s in range(TOP_K):
            _row_copy(ys_ref, pos_ref[TOP_K * (base + r) + s], buf_ref.at[s], r, sem).wait()
        return carry

    lax.fori_loop(0, tm, fetch, 0, unroll=DMA_UNROLL)
    lax.fori_loop(0, tm, done, 0, unroll=DMA_UNROLL)
    meta = meta_ref[...]
    y = meta[:, META_G:META_G + 1] * buf_ref[0] + meta[:, META_G + 1:META_G + 2] * buf_ref[1]
    o_ref[...] = x_ref[...] + _mod_chunks(mod_ref, False)[5] * y


def _combine(x, meta, mods, ys, pos, *, l, tm):
    bsz, n_tok, d = x.shape
    tok = lambda b, i, *_: (b, i, 0)
    return pl.pallas_call(
        _combine_kernel,
        grid_spec=pltpu.PrefetchScalarGridSpec(
            num_scalar_prefetch=1,
            grid=(bsz, n_tok // tm),
            in_specs=[
                pl.BlockSpec((None, tm, d), tok),
                pl.BlockSpec((None, tm, META_LANES), tok),
                _mod_spec(l, d),
                pl.BlockSpec(memory_space=pl.ANY),
            ],
            out_specs=pl.BlockSpec((None, tm, d), tok),
            scratch_shapes=[pltpu.VMEM((TOP_K, tm, d), F32), pltpu.SemaphoreType.DMA(())],
        ),
        out_shape=jax.ShapeDtypeStruct(x.shape, F32),
        compiler_params=_cparams(("arbitrary", "arbitrary")),
        name="combine",
    )(pos, x, meta, mods, ys)


def _moe(x, mods, g2, w_router, w_gate, w_up, w_down, *, l, fi):
    bsz, n_tok, d = x.shape
    tm = MOE_TM
    n_all = bsz * n_tok
    n_tiles = TOP_K * n_all // tm + N_EXPERTS
    wr = jnp.pad(w_router, ((0, 0), (0, 0), (0, META_LANES - N_EXPERTS)))
    h2, meta, counts = _route(x, mods, g2, wr, l=l, fi=fi, tm=ROUTE_TM)

    meta2 = meta.reshape(n_all, META_LANES)
    expert = meta2[:, META_E:META_E + TOP_K].astype(I32)
    rank = meta2[:, META_RANK:META_RANK + TOP_K].astype(I32)
    cnt = counts[0, :N_EXPERTS].astype(I32)
    tiles = (cnt + tm - 1) // tm
    tile_end = jnp.cumsum(tiles)
    row_start = (tile_end - tiles) * tm
    expert_ids = jnp.arange(N_EXPERTS, dtype=I32)
    pos = (rank + jnp.sum(jnp.where(expert[..., None] == expert_ids, row_start, 0), axis=-1)).reshape(-1)
    n_used = tile_end[-1:]
    tile_ids = jnp.arange(n_tiles, dtype=I32)
    tile_expert = jnp.sum(tile_ids[:, None] >= tile_end[None, :], axis=1).astype(I32)
    last_expert = jnp.max(jnp.where(tiles > 0, expert_ids, 0))
    tile_expert = jnp.minimum(tile_expert, last_expert)
    tile_first = jnp.any((tile_ids[:, None] == (tile_end - tiles)[None, :]) & (tiles > 0)[None, :],
                         axis=1).astype(I32)
    pad_start = row_start + cnt
    pad_count = tiles * tm - cnt

    free_tile = n_used * (tm // ROUTE_TM)
    xs = _dispatch(h2.reshape(n_all, d), pos, pad_start, pad_count, free_tile, n_tiles * tm, tm=ROUTE_TM)
    ys = _gmoe(xs, tile_expert, tile_first, n_used, w_gate, w_up, w_down, fi=fi, tm=tm, tf=512)
    return _combine(x, meta, mods, ys, pos, l=l, tm=ROUTE_TM)


N_ROW_OFF = 2 * NA_ROWS - 1
N_COL_OFF = 2 * NA_COLS - 1


def _bias_cells(rows):
    n_blocks = rows // ROW_BLOCK
    row_off = np.zeros((3, ROW_BLOCK, KEY_ROWS), np.int64)
    row_in = np.zeros((3, ROW_BLOCK, KEY_ROWS), bool)
    for kind, blk in enumerate((0, 1, n_blocks - 1)):
        win = int(np.clip(blk * ROW_BLOCK - NA_ROWS // 2, 0, rows - KEY_ROWS))
        r = blk * ROW_BLOCK + np.arange(ROW_BLOCK)[:, None]
        kr = win + np.arange(KEY_ROWS)[None, :]
        first = np.clip(r - NA_ROWS // 2, 0, rows - NA_ROWS)
        row_in[kind] = (kr >= first) & (kr < first + NA_ROWS)
        row_off[kind] = np.clip(kr - r + NA_ROWS - 1, 0, N_ROW_OFF - 1)
    return row_off, row_in


def _bias_kernel(rpb_ref, o_ref, toep_ref, *, rows):
    l, h, kind = pl.program_id(0), pl.program_id(1), pl.program_id(2)
    shape = (GRID_W, 2 * GRID_W)
    lane = lax.broadcasted_iota(I32, shape, 1)

    @pl.when(kind == 0)
    def _():
        qc = lax.broadcasted_iota(I32, shape, 0)
        kc = lane % GRID_W
        start = jnp.clip(qc - NA_COLS // 2, 0, GRID_W - NA_COLS)
        col_in = (kc >= start) & (kc < start + NA_COLS)
        col_off = jnp.clip(kc - qc, -(NA_COLS - 1), NA_COLS - 1) + (NA_COLS - 1)
        base = (l * C_HEADS + h) * (N_ROW_OFF * N_COL_OFF)
        for dr in range(N_ROW_OFF):
            t = jnp.full(shape, NEG_INF, F32)
            for dc in range(N_COL_OFF):
                t = jnp.where(col_off == dc, rpb_ref[base + dr * N_COL_OFF + dc], t)
            toep_ref[dr] = jnp.where(col_in, t, NEG_INF)

    row_off, row_in = _bias_cells(rows)
    masked = jnp.full(shape, NEG_INF, F32)
    for k in range(3):
        @pl.when(kind == k)
        def _():
            for i in range(ROW_BLOCK):
                for jp in range(KEY_ROWS // 2):
                    halves = [toep_ref[int(row_off[k, i, jj])] if row_in[k, i, jj] else masked
                              for jj in (2 * jp, 2 * jp + 1)]
                    o_ref[i * GRID_W:(i + 1) * GRID_W, jp * 2 * GRID_W:(jp + 1) * 2 * GRID_W] = (
                        jnp.where(lane < GRID_W, halves[0], halves[1]))


def _attention_bias(rpb, rows):
    depth = rpb.shape[0]
    assert KEY_ROWS % 2 == 0
    return pl.pallas_call(
        functools.partial(_bias_kernel, rows=rows),
        grid=(depth, C_HEADS, 3),
        in_specs=[pl.BlockSpec(memory_space=pltpu.SMEM)],
        out_specs=pl.BlockSpec((None, None, None, ROW_BLOCK * GRID_W, KEY_ROWS * GRID_W),
                               lambda l, h, k: (l, k, h, 0, 0)),
        out_shape=jax.ShapeDtypeStruct(
            (depth, 3, C_HEADS, ROW_BLOCK * GRID_W, KEY_ROWS * GRID_W), F32),
        scratch_shapes=[pltpu.VMEM((N_ROW_OFF, GRID_W, 2 * GRID_W), F32)],
        compiler_params=_cparams(("arbitrary", "arbitrary", "arbitrary")),
        name="attention_bias",
    )(rpb.astype(F32).reshape(-1))


def _block_diag(blocks):
    g, m, n = blocks.shape[-3:]
    eye = jnp.eye(g, dtype=blocks.dtype)
    out = blocks[..., :, :, None, :] * eye[:, None, :, None]
    return out.reshape(blocks.shape[:-3] + (g * m, g * n))


def kernel(x, c, ctx, c_ctx, w_mod, b_mod, norm1_g, norm2_g, w_in, w_out, a_vn_g, a_vn_b, a_ws, a_bs,
           b_wpool, b_scale, c_qn_g, c_kn_g, c_rpb, ffn_w_gate, ffn_w_up, ffn_w_down,
           moe_w_router, moe_w_gate, moe_w_up, moe_w_down):
    bsz, seq_len, d = x.shape
    depth = w_mod.shape[0]
    rows = seq_len // GRID_W
    assert bsz <= CTX_MOD_ROW

    cond = jnp.concatenate([c, jnp.zeros((CTX_MOD_ROW - bsz, d), F32), c_ctx[None, :],
                            jnp.zeros((MOD_ROWS - CTX_MOD_ROW - 1, d), F32)], axis=0)
    mods = _modulation(cond, w_mod, b_mod)
    hsum = jnp.asarray(np.kron(np.eye(C_HEADS), np.full((HEAD_DIM, HEAD_DIM), 1.0 / HEAD_DIM)), BF16)

    g1 = norm1_g.reshape(depth, 1, d)
    g2 = norm2_g.reshape(depth, 1, d)
    w_in_b = w_in.astype(BF16)
    qg = (jnp.tile(c_qn_g, (1, C_HEADS)) * ATTN_SCALE).reshape(depth, 1, C_WIDTH)
    kg = jnp.tile(c_kn_g, (1, C_HEADS)).reshape(depth, 1, C_WIDTH)
    mix_w = (a_vn_g.reshape(depth, 1, A_WIDTH), a_vn_b.reshape(depth, 1, A_WIDTH),
             a_ws.astype(BF16),
             jnp.repeat(jnp.swapaxes(a_bs, 1, 2), HEAD_DIM, axis=2),
             _block_diag(b_wpool).astype(BF16),
             b_scale.reshape(depth, 1, B_WIDTH),
             w_out.astype(BF16))
    bias = _attention_bias(c_rpb, rows)

    xc = ctx
    for l in range(depth):
        last = l == depth - 1
        fi = l // 2
        uv_c, z_c, q_c, k_c, v_c = _in_proj(xc, mods, g1, w_in_b, hsum, qg, kg,
                                            l=l, is_ctx=True, tm=xc.shape[1])
        uv, z, q, k, v = _in_proj(x, mods, g1, w_in_b, hsum, qg, kg, l=l, is_ctx=False, tm=512)
        x = _mix(x, uv, z, q, k, v, k_c, v_c, bias, mods, *mix_w, l=l)

        ffn_w = (ffn_w_gate, ffn_w_up, ffn_w_down)
        if l % 2 == 0:
            x = _ffn(x, mods, g2, *ffn_w, l=l, fi=fi, is_ctx=False, tm=1024, tf=256)
        else:
            x = _moe(x, mods, g2, moe_w_router, moe_w_gate, moe_w_up, moe_w_down, l=l, fi=fi)

        if not last:
            xc = _mix_ctx(xc, uv_c, z_c, q_c, k_c, v_c, mods, *mix_w, l=l)
            n_ctx = xc.shape[1]
            if l % 2 == 0:
                xc = _ffn(xc.reshape(1, bsz * n_ctx, d), mods, g2, *ffn_w, l=l, fi=fi,
                          is_ctx=True, tm=bsz * n_ctx, tf=256).reshape(bsz, n_ctx, d)
            else:
                raise NotImplementedError("a context-stream MoE layer only occurs for depth > 2")
    return x
```

```python
import functools

import numpy as np
import jax
import jax.numpy as jnp
from jax import lax
from jax.experimental import pallas as pl
from jax.experimental.pallas import tpu as pltpu

F32 = jnp.float32
BF16 = jnp.bfloat16
I32 = jnp.int32

D_MODEL = 1024
GRID_W = 64
HEAD_DIM = 64
A_WIDTH = 256
A_GROUPS = 4
CHUNK = 128
B_WIDTH = 256
POOL_WINDOWS = (2, 4, 8, 16)
POOL_HALO = 8
C_WIDTH = 512
C_HEADS = 8
NA_ROWS = 8
NA_COLS = 16
ATTN_SCALE = HEAD_DIM ** -0.5
IN_COLS = 2 * A_WIDTH + B_WIDTH + 3 * C_WIDTH
N_EXPERTS = 8
TOP_K = 2
EPS = 1e-6
NEG_INF = -1e30

ROW_BLOCK = 4
KEY_ROWS = ROW_BLOCK + NA_ROWS
assert ROW_BLOCK >= NA_ROWS // 2
MOD_ROWS = 8
CTX_MOD_ROW = 4
META_LANES = 128
META_E, META_G, META_RANK = 0, 2, 4
MOE_TM = 512
ROUTE_TM = 512
DMA_UNROLL = 8
SUBLANES = 8

VMEM_LIMIT = 60 * 1024 * 1024


def _cparams(sem, **kw):
    return pltpu.CompilerParams(dimension_semantics=sem, vmem_limit_bytes=VMEM_LIMIT, **kw)


def _dot(a, b):
    return jnp.dot(a, b, preferred_element_type=F32)


def _silu(t):
    return t / (1.0 + jnp.exp(-t))


def _gelu(t):
    return 0.5 * t * (1.0 + lax.erf(t * np.float32(np.sqrt(0.5))))


def _norm_mod(x, g, scale, shift):
    y = x * lax.rsqrt(jnp.mean(x * x, axis=-1, keepdims=True) + EPS) * g
    return y * (1.0 + scale) + shift


def _mod_chunks(mod_ref, is_ctx):
    row = CTX_MOD_ROW if is_ctx else pl.program_id(0)
    m = mod_ref[pl.ds(row, 1), :]
    d = m.shape[1] // 6
    return [m[:, k * d:(k + 1) * d] for k in range(6)]


def _mod_spec(l, d):
    return pl.BlockSpec((None, MOD_ROWS, 6 * d), lambda *_: (l, 0, 0))


def _layer_spec(l, shape):
    zeros = (0,) * len(shape)
    return pl.BlockSpec((None,) + tuple(shape), lambda *_: (l,) + zeros)


def _mod_kernel(cond_ref, w_ref, b_ref, o_ref):
    s = _silu(cond_ref[...])
    o_ref[...] = _dot(s.astype(BF16), w_ref[...].astype(BF16)) + b_ref[...]


def _modulation(cond, w_mod, b_mod):
    depth, d, cols = w_mod.shape
    tn = 1536
    return pl.pallas_call(
        _mod_kernel,
        grid=(depth, cols // tn),
        in_specs=[
            pl.BlockSpec((MOD_ROWS, d), lambda l, j: (0, 0)),
            pl.BlockSpec((None, d, tn), lambda l, j: (l, 0, j)),
            pl.BlockSpec((None, 1, tn), lambda l, j: (l, 0, j)),
        ],
        out_specs=pl.BlockSpec((None, MOD_ROWS, tn), lambda l, j: (l, 0, j)),
        out_shape=jax.ShapeDtypeStruct((depth, MOD_ROWS, cols), F32),
        compiler_params=_cparams(("arbitrary", "arbitrary")),
        name="modulation",
    )(cond, w_mod, b_mod.reshape(depth, 1, cols))


def _head_rms(t, hsum):
    ms = _dot((t * t).astype(BF16), hsum)
    return t * lax.rsqrt(ms + EPS)


def _in_proj_kernel(x_ref, mod_ref, g_ref, w_ref, hs_ref, qg_ref, kg_ref,
                    uv_ref, z_ref, q_ref, kt_ref, v_ref, *, is_ctx):
    shift, scale = _mod_chunks(mod_ref, is_ctx)[:2]
    h = _norm_mod(x_ref[...], g_ref[...], scale, shift)
    p = _dot(h.astype(BF16), w_ref[...])
    o = 2 * A_WIDTH
    uv_ref[...] = p[:, :o].astype(BF16)
    z_ref[...] = p[:, o:o + B_WIDTH]
    o += B_WIDTH
    hs = hs_ref[...]
    q_ref[...] = (_head_rms(p[:, o:o + C_WIDTH], hs) * qg_ref[...]).astype(BF16)
    o += C_WIDTH
    kt_ref[...] = (_head_rms(p[:, o:o + C_WIDTH], hs) * kg_ref[...]).T.astype(BF16)
    o += C_WIDTH
    v_ref[...] = p[:, o:o + C_WIDTH].astype(BF16)


def _in_proj(x, mods, g1, w_in, hsum, qg, kg, *, l, is_ctx, tm):
    bsz, n_tok, d = x.shape
    const = lambda b, i: (0, 0)
    tok = lambda b, i: (b, i, 0)

    def out(width, dtype):
        return (pl.BlockSpec((None, tm, width), tok),
                jax.ShapeDtypeStruct((bsz, n_tok, width), dtype))

    keys_t = (pl.BlockSpec((None, C_WIDTH, tm), lambda b, i: (b, 0, i)),
              jax.ShapeDtypeStruct((bsz, C_WIDTH, n_tok), BF16))
    outs = [out(2 * A_WIDTH, BF16), out(B_WIDTH, F32), out(C_WIDTH, BF16),
            keys_t, out(C_WIDTH, BF16)]
    return pl.pallas_call(
        functools.partial(_in_proj_kernel, is_ctx=is_ctx),
        grid=(bsz, n_tok // tm),
        in_specs=[
            pl.BlockSpec((None, tm, d), tok),
            _mod_spec(l, d),
            _layer_spec(l, (1, d)),
            _layer_spec(l, (d, IN_COLS)),
            pl.BlockSpec((C_WIDTH, C_WIDTH), const),
            _layer_spec(l, (1, C_WIDTH)),
            _layer_spec(l, (1, C_WIDTH)),
        ],
        out_specs=[o[0] for o in outs],
        out_shape=[o[1] for o in outs],
        compiler_params=_cparams(("arbitrary", "arbitrary")),
        name="in_proj_ctx" if is_ctx else "in_proj",
    )(x, mods, g1, w_in, hsum, qg, kg)


def _mixer_a(uv, vn_g, vn_b, ws_ref, bs_full):
    n_tok = uv.shape[0]
    u = _gelu(uv[:, :A_WIDTH])
    v = _gelu(uv[:, A_WIDTH:])
    vc = v - jnp.mean(v, axis=-1, keepdims=True)
    var = jnp.mean(vc * vc, axis=-1, keepdims=True)
    v = (vc * lax.rsqrt(var + EPS) * vn_g + vn_b).astype(BF16)
    lane_group = lax.broadcasted_iota(I32, (CHUNK, A_WIDTH), 1) // HEAD_DIM
    outs = []
    for c in range(n_tok // CHUNK):
        v_c = v[c * CHUNK:(c + 1) * CHUNK]
        mixed = bs_full
        for g in range(A_GROUPS):
            mixed = mixed + jnp.where(lane_group == g, _dot(ws_ref[g], v_c), 0.0)
        outs.append(u[c * CHUNK:(c + 1) * CHUNK] * mixed)
    return jnp.concatenate(outs, axis=0) if len(outs) > 1 else outs[0]


def _mixer_b(zh_ref, t0, n_tok, seq_len, wpool_bd, b_scale):
    def sh(d):
        return zh_ref[POOL_HALO + d:POOL_HALO + d + n_tok, :]

    z = sh(0)
    s = sh(-1) + z
    sums = [s]
    for win in POOL_WINDOWS[1:]:
        half = win // 2
        for d in range(-half, -half // 2):
            s = s + sh(d)
        for d in range(half // 2, half):
            s = s + sh(d)
        sums.append(s)
    lane_group = lax.broadcasted_iota(I32, (n_tok, B_WIDTH), 1) // (B_WIDTH // len(POOL_WINDOWS))
    pos = t0 + lax.broadcasted_iota(I32, (n_tok, B_WIDTH), 0)
    half = jnp.left_shift(1, lane_group)
    cnt = jnp.minimum(pos + half, seq_len) - jnp.maximum(pos - half, 0)
    total = sums[0]
    for g in range(1, len(POOL_WINDOWS)):
        total = jnp.where(lane_group == g, sums[g], total)
    y = total / cnt.astype(F32) - z
    return _dot(y.astype(BF16), wpool_bd) * b_scale


def _fill_halo(zh_ref, z_ref, t0, n_tok, seq_len):
    zh_ref[POOL_HALO:POOL_HALO + n_tok, :] = z_ref[pl.ds(t0, n_tok), :]
    lo = jnp.maximum(t0 - POOL_HALO, 0)
    hi = jnp.minimum(t0 + n_tok, seq_len - POOL_HALO)
    lo = pl.multiple_of(lo, POOL_HALO)
    hi = pl.multiple_of(hi, POOL_HALO)
    before = z_ref[pl.ds(lo, POOL_HALO), :]
    after = z_ref[pl.ds(hi, POOL_HALO), :]
    zh_ref[0:POOL_HALO, :] = jnp.where(t0 > 0, before, 0.0)
    zh_ref[POOL_HALO + n_tok:, :] = jnp.where(t0 + n_tok < seq_len, after, 0.0)


def _attend(q_h, kt_h, v_h, bias_h, kct_h, vc_h):
    s = _dot(q_h, kt_h)
    if bias_h is not None:
        s = s + bias_h
    m = jnp.max(s, axis=-1, keepdims=True)
    if kct_h is not None:
        sc = _dot(q_h, kct_h)
        m = jnp.maximum(m, jnp.max(sc, axis=-1, keepdims=True))
        pc = jnp.exp(sc - m)
    p = jnp.exp(s - m)
    den = jnp.sum(p, axis=-1, keepdims=True)
    o = _dot(p.astype(BF16), v_h)
    if kct_h is not None:
        den = den + jnp.sum(pc, axis=-1, keepdims=True)
        o = o + _dot(pc.astype(BF16), vc_h)
    return o / den


def _mix_kernel(x_ref, uv_ref, z_ref, q_ref, kt_ref, v_ref, kct_ref, vc_ref, bias_ref, mod_ref,
                vn_g_ref, vn_b_ref, ws_ref, bs_ref, wpool_ref, bscale_ref, wout_ref,
                o_ref, zh_ref, mix_ref, *, seq_len):
    gate1 = _mod_chunks(mod_ref, False)[2]
    j = pl.program_id(1)
    n_tok = ROW_BLOCK * GRID_W
    t0 = pl.multiple_of(j * n_tok, n_tok)
    rows = seq_len // GRID_W

    a = _mixer_a(uv_ref[...].astype(F32), vn_g_ref[...], vn_b_ref[...], ws_ref, bs_ref[...])
    mix_ref[:, :A_WIDTH] = a.astype(BF16)

    _fill_halo(zh_ref, z_ref, t0, n_tok, seq_len)
    bmix = _mixer_b(zh_ref, t0, n_tok, seq_len, wpool_ref[...], bscale_ref[...])
    mix_ref[:, A_WIDTH:A_WIDTH + B_WIDTH] = bmix.astype(BF16)

    win_row = jnp.clip(j * ROW_BLOCK - NA_ROWS // 2, 0, rows - KEY_ROWS)
    koff = pl.multiple_of(win_row * GRID_W, 2 * GRID_W)
    c_off = A_WIDTH + B_WIDTH
    for h in range(C_HEADS):
        hs = slice(h * HEAD_DIM, (h + 1) * HEAD_DIM)
        o = _attend(q_ref[:, hs],
                    kt_ref[hs, pl.ds(koff, KEY_ROWS * GRID_W)],
                    v_ref[pl.ds(koff, KEY_ROWS * GRID_W), hs],
                    bias_ref[h], kct_ref[hs, :], vc_ref[:, hs])
        mix_ref[:, c_off + h * HEAD_DIM:c_off + (h + 1) * HEAD_DIM] = o.astype(BF16)

    y = _dot(mix_ref[...], wout_ref[...])
    o_ref[...] = x_ref[...] + gate1 * y


def _mix(x, uv, z, q, k, v, kc, vc, bias, mods, vn_g, vn_b, ws, bs_full, wpool_bd, b_scale, w_out, *, l):
    bsz, seq_len, d = x.shape
    n_ctx = vc.shape[1]
    n_tok = ROW_BLOCK * GRID_W
    n_blocks = seq_len // n_tok
    assert ROW_BLOCK % 2 == 0 and (NA_ROWS // 2) % 2 == 0 and (seq_len // GRID_W - KEY_ROWS) % 2 == 0
    tok = lambda b, j: (b, j, 0)
    full = lambda b, j: (b, 0, 0)
    c2 = lambda b, j: (0, 0)
    c3 = lambda b, j: (0, 0, 0)

    def bias_map(b, j):
        kind = jnp.where(j == 0, 0, jnp.where(j == n_blocks - 1, 2, 1))
        return (l, kind, 0, 0, 0)

    return pl.pallas_call(
        functools.partial(_mix_kernel, seq_len=seq_len),
        grid=(bsz, n_blocks),
        in_specs=[
            pl.BlockSpec((None, n_tok, d), tok),
            pl.BlockSpec((None, n_tok, 2 * A_WIDTH), tok),
            pl.BlockSpec((None, seq_len, B_WIDTH), full),
            pl.BlockSpec((None, n_tok, C_WIDTH), tok),
            pl.BlockSpec((None, C_WIDTH, seq_len), full),
            pl.BlockSpec((None, seq_len, C_WIDTH), full),
            pl.BlockSpec((None, C_WIDTH, n_ctx), full),
            pl.BlockSpec((None, n_ctx, C_WIDTH), full),
            pl.BlockSpec((None, None, C_HEADS, n_tok, KEY_ROWS * GRID_W), bias_map),
            _mod_spec(l, d),
            _layer_spec(l, (1, A_WIDTH)),
            _layer_spec(l, (1, A_WIDTH)),
            _layer_spec(l, (A_GROUPS, CHUNK, CHUNK)),
            _layer_spec(l, (CHUNK, A_WIDTH)),
            _layer_spec(l, (B_WIDTH, B_WIDTH)),
            _layer_spec(l, (1, B_WIDTH)),
            _layer_spec(l, (d, d)),
        ],
        out_specs=pl.BlockSpec((None, n_tok, d), tok),
        out_shape=jax.ShapeDtypeStruct(x.shape, F32),
        scratch_shapes=[
            pltpu.VMEM((n_tok + 2 * POOL_HALO, B_WIDTH), F32),
            pltpu.VMEM((n_tok, d), BF16),
        ],
        compiler_params=_cparams(("arbitrary", "arbitrary")),
        name="mix",
    )(x, uv, z, q, k, v, kc, vc, bias, mods, vn_g, vn_b, ws, bs_full, wpool_bd, b_scale, w_out)


def _mix_ctx_kernel(x_ref, uv_ref, z_ref, q_ref, kt_ref, v_ref, mod_ref,
                    vn_g_ref, vn_b_ref, ws_ref, bs_ref, wpool_ref, bscale_ref, wout_ref,
                    o_ref, zh_ref, mix_ref):
    gate1 = _mod_chunks(mod_ref, True)[2]
    n_tok = x_ref.shape[0]
    a = _mixer_a(uv_ref[...].astype(F32), vn_g_ref[...], vn_b_ref[...], ws_ref, bs_ref[...])
    mix_ref[:, :A_WIDTH] = a.astype(BF16)

    zero = jnp.zeros((POOL_HALO, B_WIDTH), F32)
    zh_ref[0:POOL_HALO, :] = zero
    zh_ref[POOL_HALO:POOL_HALO + n_tok, :] = z_ref[...]
    zh_ref[POOL_HALO + n_tok:, :] = zero
    bmix = _mixer_b(zh_ref, 0, n_tok, n_tok, wpool_ref[...], bscale_ref[...])
    mix_ref[:, A_WIDTH:A_WIDTH + B_WIDTH] = bmix.astype(BF16)

    c_off = A_WIDTH + B_WIDTH
    for h in range(C_HEADS):
        hs = slice(h * HEAD_DIM, (h + 1) * HEAD_DIM)
        o = _attend(q_ref[:, hs], kt_ref[hs, :], v_ref[:, hs], None, None, None)
        mix_ref[:, c_off + h * HEAD_DIM:c_off + (h + 1) * HEAD_DIM] = o.astype(BF16)

    y = _dot(mix_ref[...], wout_ref[...])
    o_ref[...] = x_ref[...] + gate1 * y


def _mix_ctx(x, uv, z, q, k, v, mods, vn_g, vn_b, ws, bs_full, wpool_bd, b_scale, w_out, *, l):
    bsz, n_tok, d = x.shape
    full = lambda b: (b, 0, 0)
    c2 = lambda b: (0, 0)
    c3 = lambda b: (0, 0, 0)
    return pl.pallas_call(
        _mix_ctx_kernel,
        grid=(bsz,),
        in_specs=[
            pl.BlockSpec((None, n_tok, d), full),
            pl.BlockSpec((None, n_tok, 2 * A_WIDTH), full),
            pl.BlockSpec((None, n_tok, B_WIDTH), full),
            pl.BlockSpec((None, n_tok, C_WIDTH), full),
            pl.BlockSpec((None, C_WIDTH, n_tok), full),
            pl.BlockSpec((None, n_tok, C_WIDTH), full),
            _mod_spec(l, d),
            _layer_spec(l, (1, A_WIDTH)),
            _layer_spec(l, (1, A_WIDTH)),
            _layer_spec(l, (A_GROUPS, CHUNK, CHUNK)),
            _layer_spec(l, (CHUNK, A_WIDTH)),
            _layer_spec(l, (B_WIDTH, B_WIDTH)),
            _layer_spec(l, (1, B_WIDTH)),
            _layer_spec(l, (d, d)),
        ],
        out_specs=pl.BlockSpec((None, n_tok, d), full),
        out_shape=jax.ShapeDtypeStruct(x.shape, F32),
        scratch_shapes=[
            pltpu.VMEM((n_tok + 2 * POOL_HALO, B_WIDTH), F32),
            pltpu.VMEM((n_tok, d), BF16),
        ],
        compiler_params=_cparams(("arbitrary",)),
        name="mix_ctx",
    )(x, uv, z, q, k, v, mods, vn_g, vn_b, ws, bs_full, wpool_bd, b_scale, w_out)


def _swiglu_tile(h_ref, acc_ref, act_ref, wg_ref, wu_ref, wd_ref, tf):
    nf = wg_ref.shape[1] // tf

    def up(f):
        cols = pl.ds(pl.multiple_of(f * tf, tf), tf)
        h = h_ref[...]
        act = _silu(_dot(h, wg_ref[:, cols])) * _dot(h, wu_ref[:, cols])
        act_ref[f % 2] = act.astype(BF16)

    def down(f):
        rows = pl.ds(pl.multiple_of(f * tf, tf), tf)
        acc_ref[...] += _dot(act_ref[f % 2], wd_ref[rows, :])

    def body(f, carry):
        down(f - 1)
        up(f)
        return carry

    up(0)
    lax.fori_loop(1, nf, body, 0)
    down(nf - 1)


def _resident_spec(shape, index_map):
    return pl.BlockSpec(shape, index_map, pipeline_mode=pl.Buffered(1))


def _ffn_kernel(x_ref, mod_ref, g_ref, wg_ref, wu_ref, wd_ref, o_ref, h_ref, acc_ref, act_ref,
                *, is_ctx, tf):
    _, _, _, shift, scale, gate = _mod_chunks(mod_ref, is_ctx)
    h = _norm_mod(x_ref[...], g_ref[...], scale, shift)
    h_ref[...] = h.astype(BF16)
    acc_ref[...] = jnp.zeros_like(acc_ref)
    _swiglu_tile(h_ref, acc_ref, act_ref, wg_ref, wu_ref, wd_ref, tf)
    o_ref[...] = x_ref[...] + gate * acc_ref[...]


def _ffn(x, mods, g2, w_gate, w_up, w_down, *, l, fi, is_ctx, tm, tf):
    bsz, n_tok, d = x.shape
    d_ff = w_gate.shape[-1]
    tok = lambda b, i: (b, i, 0)
    layer = lambda b, i: (fi, 0, 0)
    return pl.pallas_call(
        functools.partial(_ffn_kernel, is_ctx=is_ctx, tf=tf),
        grid=(bsz, n_tok // tm),
        in_specs=[
            pl.BlockSpec((None, tm, d), tok),
            _mod_spec(l, d),
            _layer_spec(l, (1, d)),
            _resident_spec((None, d, d_ff), layer),
            _resident_spec((None, d, d_ff), layer),
            _resident_spec((None, d_ff, d), layer),
        ],
        out_specs=pl.BlockSpec((None, tm, d), tok),
        out_shape=jax.ShapeDtypeStruct(x.shape, F32),
        scratch_shapes=[pltpu.VMEM((tm, d), BF16), pltpu.VMEM((tm, d), F32),
                        pltpu.VMEM((2, tm, tf), BF16)],
        compiler_params=_cparams(("arbitrary", "arbitrary")),
        name="ffn_ctx" if is_ctx else "ffn",
    )(x, mods, g2, w_gate, w_up, w_down)


def _route_kernel(x_ref, mod_ref, g_ref, wr_ref, h_ref, meta_ref, cnt_ref, run_ref):
    @pl.when((pl.program_id(0) == 0) & (pl.program_id(1) == 0))
    def _():
        run_ref[...] = jnp.zeros_like(run_ref)

    _, _, _, shift, scale, _ = _mod_chunks(mod_ref, False)
    h = _norm_mod(x_ref[...], g_ref[...], scale, shift)
    h_ref[...] = h
    logits = jnp.dot(h, wr_ref[...], preferred_element_type=F32, precision=lax.Precision.HIGHEST)
    tm = logits.shape[0]
    lane = lax.broadcasted_iota(I32, logits.shape, 1)
    logits = jnp.where(lane < N_EXPERTS, logits, -jnp.inf)
    m1 = jnp.max(logits, axis=-1, keepdims=True)
    i1 = jnp.min(jnp.where(logits == m1, lane, META_LANES), axis=-1, keepdims=True)
    rest = jnp.where(lane == i1, -jnp.inf, logits)
    m2 = jnp.max(rest, axis=-1, keepdims=True)
    i2 = jnp.min(jnp.where(rest == m2, lane, META_LANES), axis=-1, keepdims=True)
    e2 = jnp.exp(m2 - m1)
    den = 1.0 + e2

    pick1 = lane == i1
    pick2 = lane == i2
    onehot = jnp.where(pick1, 1.0, 0.0) + jnp.where(pick2, 1.0, 0.0)
    earlier = (lax.broadcasted_iota(I32, (tm, tm), 0) > lax.broadcasted_iota(I32, (tm, tm), 1))
    before = run_ref[...] + _dot(jnp.where(earlier, 1.0, 0.0).astype(BF16), onehot.astype(BF16))
    r1 = jnp.sum(jnp.where(pick1, before, 0.0), axis=-1, keepdims=True)
    r2 = jnp.sum(jnp.where(pick2, before, 0.0), axis=-1, keepdims=True)
    run_ref[...] += jnp.sum(onehot, axis=0, keepdims=True)
    cnt_ref[...] = run_ref[...]

    meta = jnp.zeros(logits.shape, F32)
    for col, val in ((META_E, i1.astype(F32)), (META_E + 1, i2.astype(F32)),
                     (META_G, 1.0 / den), (META_G + 1, e2 / den),
                     (META_RANK, r1), (META_RANK + 1, r2)):
        meta = jnp.where(lane == col, val, meta)
    meta_ref[...] = meta


def _route(x, mods, g2, w_router_pad, *, l, fi, tm):
    bsz, n_tok, d = x.shape
    tok = lambda b, i: (b, i, 0)
    return pl.pallas_call(
        _route_kernel,
        grid=(bsz, n_tok // tm),
        in_specs=[
            pl.BlockSpec((None, tm, d), tok),
            _mod_spec(l, d),
            _layer_spec(l, (1, d)),
            _layer_spec(fi, (d, META_LANES)),
        ],
        out_specs=[
            pl.BlockSpec((None, tm, d), tok),
            pl.BlockSpec((None, tm, META_LANES), tok),
            pl.BlockSpec((1, META_LANES), lambda b, i: (0, 0)),
        ],
        out_shape=[
            jax.ShapeDtypeStruct((bsz, n_tok, d), F32),
            jax.ShapeDtypeStruct((bsz, n_tok, META_LANES), F32),
            jax.ShapeDtypeStruct((1, META_LANES), F32),
        ],
        scratch_shapes=[pltpu.VMEM((1, META_LANES), F32)],
        compiler_params=_cparams(("arbitrary", "arbitrary")),
        name="route",
    )(x, mods, g2, w_router_pad)


def _row_copy(src_ref, src_row, dst_ref, dst_row, sem):
    return pltpu.make_async_copy(src_ref.at[pl.ds(src_row, 1), :], dst_ref.at[pl.ds(dst_row, 1), :], sem)


def _dispatch_kernel(pos_ref, pad_start_ref, pad_count_ref, free_tile_ref, h_ref, xs_ref,
                     zero_ref, sem, zsem):
    tm = h_ref.shape[0]
    step = pl.program_id(0)

    @pl.when(step == 0)
    def _():
        zero_ref[...] = jnp.zeros_like(zero_ref)

        def tile_copy(t):
            return pltpu.make_async_copy(zero_ref, xs_ref.at[pl.ds(t * tm, tm), :], zsem)

        def fill_tile(t, carry):
            tile_copy(t).start()
            return carry

        def drain_tile(t, carry):
            tile_copy(t).wait()
            return carry

        n_tiles = xs_ref.shape[0] // tm
        lax.fori_loop(free_tile_ref[0], n_tiles, fill_tile, 0)
        lax.fori_loop(free_tile_ref[0], n_tiles, drain_tile, 0)
        for e in range(N_EXPERTS):
            start, count = pad_start_ref[e], pad_count_ref[e]
            head = (-start) & (SUBLANES - 1)
            chunks = [(i < head, _row_copy(zero_ref, 0, xs_ref, start + i, zsem))
                      for i in range(SUBLANES - 1)]
            body_start, body = start + head, count - head
            for bit in reversed(range(SUBLANES.bit_length() - 1, (MOE_TM - 1).bit_length())):
                size = 1 << bit
                assert size <= tm
                offset = pl.multiple_of(body_start + (body - (body & (2 * size - 1))), SUBLANES)
                copy = pltpu.make_async_copy(zero_ref.at[pl.ds(0, size), :],
                                             xs_ref.at[pl.ds(offset, size), :], zsem)
                chunks.append(((body & size) != 0, copy))
            for present, copy in chunks:
                pl.when(present)(copy.start)
            for present, copy in chunks:
                pl.when(present)(copy.wait)

    base = step * tm

    def send(r, carry):
        for s in range(TOP_K):
            _row_copy(h_ref, r, xs_ref, pos_ref[TOP_K * (base + r) + s], sem).start()
        return carry

    def done(r, carry):
        for s in range(TOP_K):
            _row_copy(h_ref, r, xs_ref, pos_ref[TOP_K * (base + r) + s], sem).wait()
        return carry

    lax.fori_loop(0, tm, send, 0, unroll=DMA_UNROLL)
    lax.fori_loop(0, tm, done, 0, unroll=DMA_UNROLL)


def _dispatch(h2, pos, pad_start, pad_count, free_tile, n_rows, *, tm):
    n_tok, d = h2.shape
    return pl.pallas_call(
        _dispatch_kernel,
        grid_spec=pltpu.PrefetchScalarGridSpec(
            num_scalar_prefetch=4,
            grid=(n_tok // tm,),
            in_specs=[pl.BlockSpec((tm, d), lambda i, *_: (i, 0))],
            out_specs=pl.BlockSpec(memory_space=pl.ANY),
            scratch_shapes=[pltpu.VMEM((tm, d), F32), pltpu.SemaphoreType.DMA(()),
                            pltpu.SemaphoreType.DMA(())],
        ),
        out_shape=jax.ShapeDtypeStruct((n_rows, d), F32),
        compiler_params=_cparams(("arbitrary",), has_side_effects=True),
        name="dispatch",
    )(pos, pad_start, pad_count, free_tile, h2)


def _gmoe_kernel(tile_expert_ref, n_used_ref, xs_ref, wg_ref, wu_ref, wd_ref, ys_ref,
                 h_ref, act_ref, *, tf):
    ys_ref[...] = jnp.zeros_like(ys_ref)

    @pl.when(pl.program_id(0) < n_used_ref[0])
    def _():
        h_ref[...] = xs_ref[...].astype(BF16)
        _swiglu_tile(h_ref, ys_ref, act_ref, wg_ref, wu_ref, wd_ref, tf)


def _gmoe(xs, tile_expert, n_used, w_gate, w_up, w_down, *, fi, tm, tf):
    n_rows, d = xs.shape
    d_ff = w_gate.shape[-1]

    def x_map(t, te, nu):
        return (jnp.maximum(jnp.minimum(t, nu[0] - 1), 0), 0)

    expert = lambda t, te, nu: (fi, te[t], 0, 0)
    return pl.pallas_call(
        functools.partial(_gmoe_kernel, tf=tf),
        grid_spec=pltpu.PrefetchScalarGridSpec(
            num_scalar_prefetch=2,
            grid=(n_rows // tm,),
            in_specs=[
                pl.BlockSpec((tm, d), x_map),
                _resident_spec((None, None, d, d_ff), expert),
                _resident_spec((None, None, d, d_ff), expert),
                _resident_spec((None, None, d_ff, d), expert),
            ],
            out_specs=pl.BlockSpec((tm, d), lambda t, te, nu: (t, 0)),
            scratch_shapes=[pltpu.VMEM((tm, d), BF16), pltpu.VMEM((2, tm, tf), BF16)],
        ),
        out_shape=jax.ShapeDtypeStruct((n_rows, d), F32),
        compiler_params=_cparams(("arbitrary",)),
        name="grouped_moe",
    )(tile_expert, n_used, xs, w_gate, w_up, w_down)


def _combine_kernel(pos_ref, x_ref, meta_ref, mod_ref, ys_ref, o_ref, buf_ref, sem):
    tm = x_ref.shape[0]
    base = (pl.program_id(0) * pl.num_programs(1) + pl.program_id(1)) * tm

    def fetch(r, carry):
        for s in range(TOP_K):
            _row_copy(ys_ref, pos_ref[TOP_K * (base + r) + s], buf_ref.at[s], r, sem).start()
        return carry

    def done(r, carry):
        for s in range(TOP_K):
            _row_copy(ys_ref, pos_ref[TOP_K * (base + r) + s], buf_ref.at[s], r, sem).wait()
        return carry

    lax.fori_loop(0, tm, fetch, 0, unroll=DMA_UNROLL)
    lax.fori_loop(0, tm, done, 0, unroll=DMA_UNROLL)
    meta = meta_ref[...]
    y = meta[:, META_G:META_G + 1] * buf_ref[0] + meta[:, META_G + 1:META_G + 2] * buf_ref[1]
    o_ref[...] = x_ref[...] + _mod_chunks(mod_ref, False)[5] * y


def _combine(x, meta, mods, ys, pos, *, l, tm):
    bsz, n_tok, d = x.shape
    tok = lambda b, i, *_: (b, i, 0)
    return pl.pallas_call(
        _combine_kernel,
        grid_spec=pltpu.PrefetchScalarGridSpec(
            num_scalar_prefetch=1,
            grid=(bsz, n_tok // tm),
            in_specs=[
                pl.BlockSpec((None, tm, d), tok),
                pl.BlockSpec((None, tm, META_LANES), tok),
                _mod_spec(l, d),
                pl.BlockSpec(memory_space=pl.ANY),
            ],
            out_specs=pl.BlockSpec((None, tm, d), tok),
            scratch_shapes=[pltpu.VMEM((TOP_K, tm, d), F32), pltpu.SemaphoreType.DMA(())],
        ),
        out_shape=jax.ShapeDtypeStruct(x.shape, F32),
        compiler_params=_cparams(("arbitrary", "arbitrary")),
        name="combine",
    )(pos, x, meta, mods, ys)


def _moe(x, mods, g2, w_router, w_gate, w_up, w_down, *, l, fi):
    bsz, n_tok, d = x.shape
    tm = MOE_TM
    n_all = bsz * n_tok
    n_tiles = TOP_K * n_all // tm + N_EXPERTS
    wr = jnp.pad(w_router, ((0, 0), (0, 0), (0, META_LANES - N_EXPERTS)))
    h2, meta, counts = _route(x, mods, g2, wr, l=l, fi=fi, tm=ROUTE_TM)

    meta2 = meta.reshape(n_all, META_LANES)
    expert = meta2[:, META_E:META_E + TOP_K].astype(I32)
    rank = meta2[:, META_RANK:META_RANK + TOP_K].astype(I32)
    cnt = counts[0, :N_EXPERTS].astype(I32)
    tiles = (cnt + tm - 1) // tm
    tile_end = jnp.cumsum(tiles)
    row_start = (tile_end - tiles) * tm
    expert_ids = jnp.arange(N_EXPERTS, dtype=I32)
    pos = (rank + jnp.sum(jnp.where(expert[..., None] == expert_ids, row_start, 0), axis=-1)).reshape(-1)
    n_used = tile_end[-1:]
    tile_ids = jnp.arange(n_tiles, dtype=I32)
    tile_expert = jnp.sum(tile_ids[:, None] >= tile_end[None, :], axis=1).astype(I32)
    last_expert = jnp.max(jnp.where(tiles > 0, expert_ids, 0))
    tile_expert = jnp.minimum(tile_expert, last_expert)
    pad_start = row_start + cnt
    pad_count = tiles * tm - cnt

    free_tile = n_used * (tm // ROUTE_TM)
    xs = _dispatch(h2.reshape(n_all, d), pos, pad_start, pad_count, free_tile, n_tiles * tm, tm=ROUTE_TM)
    ys = _gmoe(xs, tile_expert, n_used, w_gate, w_up, w_down, fi=fi, tm=tm, tf=512)
    return _combine(x, meta, mods, ys, pos, l=l, tm=ROUTE_TM)


N_ROW_OFF = 2 * NA_ROWS - 1
N_COL_OFF = 2 * NA_COLS - 1


def _bias_cells(rows):
    n_blocks = rows // ROW_BLOCK
    row_off = np.zeros((3, ROW_BLOCK, KEY_ROWS), np.int64)
    row_in = np.zeros((3, ROW_BLOCK, KEY_ROWS), bool)
    for kind, blk in enumerate((0, 1, n_blocks - 1)):
        win = int(np.clip(blk * ROW_BLOCK - NA_ROWS // 2, 0, rows - KEY_ROWS))
        r = blk * ROW_BLOCK + np.arange(ROW_BLOCK)[:, None]
        kr = win + np.arange(KEY_ROWS)[None, :]
        first = np.clip(r - NA_ROWS // 2, 0, rows - NA_ROWS)
        row_in[kind] = (kr >= first) & (kr < first + NA_ROWS)
        row_off[kind] = np.clip(kr - r + NA_ROWS - 1, 0, N_ROW_OFF - 1)
    return row_off, row_in


def _bias_kernel(rpb_ref, o_ref, toep_ref, *, rows):
    l, h, kind = pl.program_id(0), pl.program_id(1), pl.program_id(2)
    shape = (GRID_W, 2 * GRID_W)
    lane = lax.broadcasted_iota(I32, shape, 1)

    @pl.when(kind == 0)
    def _():
        qc = lax.broadcasted_iota(I32, shape, 0)
        kc = lane % GRID_W
        start = jnp.clip(qc - NA_COLS // 2, 0, GRID_W - NA_COLS)
        col_in = (kc >= start) & (kc < start + NA_COLS)
        col_off = jnp.clip(kc - qc, -(NA_COLS - 1), NA_COLS - 1) + (NA_COLS - 1)
        base = (l * C_HEADS + h) * (N_ROW_OFF * N_COL_OFF)
        for dr in range(N_ROW_OFF):
            t = jnp.full(shape, NEG_INF, F32)
            for dc in range(N_COL_OFF):
                t = jnp.where(col_off == dc, rpb_ref[base + dr * N_COL_OFF + dc], t)
            toep_ref[dr] = jnp.where(col_in, t, NEG_INF)

    row_off, row_in = _bias_cells(rows)
    masked = jnp.full(shape, NEG_INF, F32)
    for k in range(3):
        @pl.when(kind == k)
        def _():
            for i in range(ROW_BLOCK):
                for jp in range(KEY_ROWS // 2):
                    halves = [toep_ref[int(row_off[k, i, jj])] if row_in[k, i, jj] else masked
                              for jj in (2 * jp, 2 * jp + 1)]
                    o_ref[i * GRID_W:(i + 1) * GRID_W, jp * 2 * GRID_W:(jp + 1) * 2 * GRID_W] = (
                        jnp.where(lane < GRID_W, halves[0], halves[1]))


def _attention_bias(rpb, rows):
    depth = rpb.shape[0]
    assert KEY_ROWS % 2 == 0
    return pl.pallas_call(
        functools.partial(_bias_kernel, rows=rows),
        grid=(depth, C_HEADS, 3),
        in_specs=[pl.BlockSpec(memory_space=pltpu.SMEM)],
        out_specs=pl.BlockSpec((None, None, None, ROW_BLOCK * GRID_W, KEY_ROWS * GRID_W),
                               lambda l, h, k: (l, k, h, 0, 0)),
        out_shape=jax.ShapeDtypeStruct(
            (depth, 3, C_HEADS, ROW_BLOCK * GRID_W, KEY_ROWS * GRID_W), F32),
        scratch_shapes=[pltpu.VMEM((N_ROW_OFF, GRID_W, 2 * GRID_W), F32)],
        compiler_params=_cparams(("arbitrary", "arbitrary", "arbitrary")),
        name="attention_bias",
    )(rpb.astype(F32).reshape(-1))


def _block_diag(blocks):
    g, m, n = blocks.shape[-3:]
    eye = jnp.eye(g, dtype=blocks.dtype)
    out = blocks[..., :, :, None, :] * eye[:, None, :, None]
    return out.reshape(blocks.shape[:-3] + (g * m, g * n))


def kernel(x, c, ctx, c_ctx, w_mod, b_mod, norm1_g, norm2_g, w_in, w_out, a_vn_g, a_vn_b, a_ws, a_bs,
           b_wpool, b_scale, c_qn_g, c_kn_g, c_rpb, ffn_w_gate, ffn_w_up, ffn_w_down,
           moe_w_router, moe_w_gate, moe_w_up, moe_w_down):
    bsz, seq_len, d = x.shape
    depth = w_mod.shape[0]
    rows = seq_len // GRID_W
    assert bsz <= CTX_MOD_ROW

    cond = jnp.concatenate([c, jnp.zeros((CTX_MOD_ROW - bsz, d), F32), c_ctx[None, :],
                            jnp.zeros((MOD_ROWS - CTX_MOD_ROW - 1, d), F32)], axis=0)
    mods = _modulation(cond, w_mod, b_mod)
    hsum = jnp.asarray(np.kron(np.eye(C_HEADS), np.full((HEAD_DIM, HEAD_DIM), 1.0 / HEAD_DIM)), BF16)

    g1 = norm1_g.reshape(depth, 1, d)
    g2 = norm2_g.reshape(depth, 1, d)
    w_in_b = w_in.astype(BF16)
    qg = (jnp.tile(c_qn_g, (1, C_HEADS)) * ATTN_SCALE).reshape(depth, 1, C_WIDTH)
    kg = jnp.tile(c_kn_g, (1, C_HEADS)).reshape(depth, 1, C_WIDTH)
    mix_w = (a_vn_g.reshape(depth, 1, A_WIDTH), a_vn_b.reshape(depth, 1, A_WIDTH),
             a_ws.astype(BF16),
             jnp.repeat(jnp.swapaxes(a_bs, 1, 2), HEAD_DIM, axis=2),
             _block_diag(b_wpool).astype(BF16),
             b_scale.reshape(depth, 1, B_WIDTH),
             w_out.astype(BF16))
    bias = _attention_bias(c_rpb, rows)

    xc = ctx
    for l in range(depth):
        last = l == depth - 1
        fi = l // 2
        uv_c, z_c, q_c, k_c, v_c = _in_proj(xc, mods, g1, w_in_b, hsum, qg, kg,
                                            l=l, is_ctx=True, tm=xc.shape[1])
        uv, z, q, k, v = _in_proj(x, mods, g1, w_in_b, hsum, qg, kg, l=l, is_ctx=False, tm=512)
        x = _mix(x, uv, z, q, k, v, k_c, v_c, bias, mods, *mix_w, l=l)

        ffn_w = (ffn_w_gate.astype(BF16), ffn_w_up.astype(BF16), ffn_w_down.astype(BF16))
        if l % 2 == 0:
            x = _ffn(x, mods, g2, *ffn_w, l=l, fi=fi, is_ctx=False, tm=1024, tf=256)
        else:
            x = _moe(x, mods, g2, moe_w_router, moe_w_gate.astype(BF16), moe_w_up.astype(BF16),
                     moe_w_down.astype(BF16), l=l, fi=fi)

        if not last:
            xc = _mix_ctx(xc, uv_c, z_c, q_c, k_c, v_c, mods, *mix_w, l=l)
            n_ctx = xc.shape[1]
            if l % 2 == 0:
                xc = _ffn(xc.reshape(1, bsz * n_ctx, d), mods, g2, *ffn_w, l=l, fi=fi,
                          is_ctx=True, tm=bsz * n_ctx, tf=256).reshape(bsz, n_ctx, d)
            else:
                raise NotImplementedError("a context-stream MoE layer only occurs for depth > 2")
    return x
```

```python
import functools

import numpy as np
import jax
import jax.numpy as jnp
from jax import lax
from jax.experimental import pallas as pl
from jax.experimental.pallas import tpu as pltpu

F32 = jnp.float32
BF16 = jnp.bfloat16
I32 = jnp.int32

D_MODEL = 1024
GRID_W = 64
HEAD_DIM = 64
A_WIDTH = 256
A_GROUPS = 4
CHUNK = 128
B_WIDTH = 256
POOL_WINDOWS = (2, 4, 8, 16)
POOL_HALO = 8
C_WIDTH = 512
C_HEADS = 8
NA_ROWS = 8
NA_COLS = 16
ATTN_SCALE = HEAD_DIM ** -0.5
IN_COLS = 2 * A_WIDTH + B_WIDTH + 3 * C_WIDTH
N_EXPERTS = 8
TOP_K = 2
EPS = 1e-6
NEG_INF = -1e30

ROW_BLOCK = 4
KEY_ROWS = ROW_BLOCK + NA_ROWS
assert ROW_BLOCK >= NA_ROWS // 2
MOD_ROWS = 8
CTX_MOD_ROW = 4
META_LANES = 128
META_E, META_G, META_RANK = 0, 2, 4
MOE_TM = 512
ROUTE_TM = 512
DMA_UNROLL = 8
SUBLANES = 8

VMEM_LIMIT = 60 * 1024 * 1024


def _cparams(sem, **kw):
    return pltpu.CompilerParams(dimension_semantics=sem, vmem_limit_bytes=VMEM_LIMIT, **kw)


def _dot(a, b):
    return jnp.dot(a, b, preferred_element_type=F32)


def _silu(t):
    return t / (1.0 + jnp.exp(-t))


def _gelu(t):
    return 0.5 * t * (1.0 + lax.erf(t * np.float32(np.sqrt(0.5))))


def _norm_mod(x, g, scale, shift):
    y = x * lax.rsqrt(jnp.mean(x * x, axis=-1, keepdims=True) + EPS) * g
    return y * (1.0 + scale) + shift


def _mod_chunks(mod_ref, is_ctx):
    row = CTX_MOD_ROW if is_ctx else pl.program_id(0)
    m = mod_ref[pl.ds(row, 1), :]
    d = m.shape[1] // 6
    return [m[:, k * d:(k + 1) * d] for k in range(6)]


def _mod_spec(l, d):
    return pl.BlockSpec((None, MOD_ROWS, 6 * d), lambda *_: (l, 0, 0))


def _layer_spec(l, shape):
    zeros = (0,) * len(shape)
    return pl.BlockSpec((None,) + tuple(shape), lambda *_: (l,) + zeros)


def _mod_kernel(cond_ref, w_ref, b_ref, o_ref):
    s = _silu(cond_ref[...])
    o_ref[...] = _dot(s.astype(BF16), w_ref[...].astype(BF16)) + b_ref[...]


def _modulation(cond, w_mod, b_mod):
    depth, d, cols = w_mod.shape
    tn = 1536
    return pl.pallas_call(
        _mod_kernel,
        grid=(depth, cols // tn),
        in_specs=[
            pl.BlockSpec((MOD_ROWS, d), lambda l, j: (0, 0)),
            pl.BlockSpec((None, d, tn), lambda l, j: (l, 0, j)),
            pl.BlockSpec((None, 1, tn), lambda l, j: (l, 0, j)),
        ],
        out_specs=pl.BlockSpec((None, MOD_ROWS, tn), lambda l, j: (l, 0, j)),
        out_shape=jax.ShapeDtypeStruct((depth, MOD_ROWS, cols), F32),
        compiler_params=_cparams(("arbitrary", "arbitrary")),
        name="modulation",
    )(cond, w_mod, b_mod.reshape(depth, 1, cols))


def _head_rms(t, hsum):
    ms = _dot((t * t).astype(BF16), hsum)
    return t * lax.rsqrt(ms + EPS)


def _in_proj_kernel(x_ref, mod_ref, g_ref, w_ref, hs_ref, qg_ref, kg_ref,
                    uv_ref, z_ref, q_ref, kt_ref, v_ref, *, is_ctx):
    shift, scale = _mod_chunks(mod_ref, is_ctx)[:2]
    h = _norm_mod(x_ref[...], g_ref[...], scale, shift)
    p = _dot(h.astype(BF16), w_ref[...])
    o = 2 * A_WIDTH
    uv_ref[...] = p[:, :o].astype(BF16)
    z_ref[...] = p[:, o:o + B_WIDTH]
    o += B_WIDTH
    hs = hs_ref[...]
    q_ref[...] = (_head_rms(p[:, o:o + C_WIDTH], hs) * qg_ref[...]).astype(BF16)
    o += C_WIDTH
    kt_ref[...] = (_head_rms(p[:, o:o + C_WIDTH], hs) * kg_ref[...]).T.astype(BF16)
    o += C_WIDTH
    v_ref[...] = p[:, o:o + C_WIDTH].astype(BF16)


def _in_proj(x, mods, g1, w_in, hsum, qg, kg, *, l, is_ctx, tm):
    bsz, n_tok, d = x.shape
    const = lambda b, i: (0, 0)
    tok = lambda b, i: (b, i, 0)

    def out(width, dtype):
        return (pl.BlockSpec((None, tm, width), tok),
                jax.ShapeDtypeStruct((bsz, n_tok, width), dtype))

    keys_t = (pl.BlockSpec((None, C_WIDTH, tm), lambda b, i: (b, 0, i)),
              jax.ShapeDtypeStruct((bsz, C_WIDTH, n_tok), BF16))
    outs = [out(2 * A_WIDTH, BF16), out(B_WIDTH, F32), out(C_WIDTH, BF16),
            keys_t, out(C_WIDTH, BF16)]
    return pl.pallas_call(
        functools.partial(_in_proj_kernel, is_ctx=is_ctx),
        grid=(bsz, n_tok // tm),
        in_specs=[
            pl.BlockSpec((None, tm, d), tok),
            _mod_spec(l, d),
            _layer_spec(l, (1, d)),
            _layer_spec(l, (d, IN_COLS)),
            pl.BlockSpec((C_WIDTH, C_WIDTH), const),
            _layer_spec(l, (1, C_WIDTH)),
            _layer_spec(l, (1, C_WIDTH)),
        ],
        out_specs=[o[0] for o in outs],
        out_shape=[o[1] for o in outs],
        compiler_params=_cparams(("arbitrary", "arbitrary")),
        name="in_proj_ctx" if is_ctx else "in_proj",
    )(x, mods, g1, w_in, hsum, qg, kg)


def _mixer_a(uv, vn_g, vn_b, ws_ref, bs_full):
    n_tok = uv.shape[0]
    u = _gelu(uv[:, :A_WIDTH])
    v = _gelu(uv[:, A_WIDTH:])
    vc = v - jnp.mean(v, axis=-1, keepdims=True)
    var = jnp.mean(vc * vc, axis=-1, keepdims=True)
    v = (vc * lax.rsqrt(var + EPS) * vn_g + vn_b).astype(BF16)
    lane_group = lax.broadcasted_iota(I32, (CHUNK, A_WIDTH), 1) // HEAD_DIM
    outs = []
    for c in range(n_tok // CHUNK):
        v_c = v[c * CHUNK:(c + 1) * CHUNK]
        mixed = bs_full
        for g in range(A_GROUPS):
            mixed = mixed + jnp.where(lane_group == g, _dot(ws_ref[g], v_c), 0.0)
        outs.append(u[c * CHUNK:(c + 1) * CHUNK] * mixed)
    return jnp.concatenate(outs, axis=0) if len(outs) > 1 else outs[0]


def _mixer_b(zh_ref, t0, n_tok, seq_len, wpool_bd, b_scale):
    def sh(d):
        return zh_ref[POOL_HALO + d:POOL_HALO + d + n_tok, :]

    z = sh(0)
    s = sh(-1) + z
    sums = [s]
    for win in POOL_WINDOWS[1:]:
        half = win // 2
        for d in range(-half, -half // 2):
            s = s + sh(d)
        for d in range(half // 2, half):
            s = s + sh(d)
        sums.append(s)
    lane_group = lax.broadcasted_iota(I32, (n_tok, B_WIDTH), 1) // (B_WIDTH // len(POOL_WINDOWS))
    pos = t0 + lax.broadcasted_iota(I32, (n_tok, B_WIDTH), 0)
    half = jnp.left_shift(1, lane_group)
    cnt = jnp.minimum(pos + half, seq_len) - jnp.maximum(pos - half, 0)
    total = sums[0]
    for g in range(1, len(POOL_WINDOWS)):
        total = jnp.where(lane_group == g, sums[g], total)
    y = total / cnt.astype(F32) - z
    return _dot(y.astype(BF16), wpool_bd) * b_scale


def _fill_halo(zh_ref, z_ref, t0, n_tok, seq_len):
    zh_ref[POOL_HALO:POOL_HALO + n_tok, :] = z_ref[pl.ds(t0, n_tok), :]
    lo = jnp.maximum(t0 - POOL_HALO, 0)
    hi = jnp.minimum(t0 + n_tok, seq_len - POOL_HALO)
    lo = pl.multiple_of(lo, POOL_HALO)
    hi = pl.multiple_of(hi, POOL_HALO)
    before = z_ref[pl.ds(lo, POOL_HALO), :]
    after = z_ref[pl.ds(hi, POOL_HALO), :]
    zh_ref[0:POOL_HALO, :] = jnp.where(t0 > 0, before, 0.0)
    zh_ref[POOL_HALO + n_tok:, :] = jnp.where(t0 + n_tok < seq_len, after, 0.0)


def _attend(q_h, kt_h, v_h, bias_h, kct_h, vc_h):
    s = _dot(q_h, kt_h)
    if bias_h is not None:
        s = s + bias_h
    m = jnp.max(s, axis=-1, keepdims=True)
    if kct_h is not None:
        sc = _dot(q_h, kct_h)
        m = jnp.maximum(m, jnp.max(sc, axis=-1, keepdims=True))
        pc = jnp.exp(sc - m)
    p = jnp.exp(s - m)
    den = jnp.sum(p, axis=-1, keepdims=True)
    o = _dot(p.astype(BF16), v_h)
    if kct_h is not None:
        den = den + jnp.sum(pc, axis=-1, keepdims=True)
        o = o + _dot(pc.astype(BF16), vc_h)
    return o / den


def _mix_kernel(x_ref, uv_ref, z_ref, q_ref, kt_ref, v_ref, kct_ref, vc_ref, bias_ref, mod_ref,
                vn_g_ref, vn_b_ref, ws_ref, bs_ref, wpool_ref, bscale_ref, wout_ref,
                o_ref, zh_ref, mix_ref, *, seq_len):
    gate1 = _mod_chunks(mod_ref, False)[2]
    j = pl.program_id(1)
    n_tok = ROW_BLOCK * GRID_W
    t0 = pl.multiple_of(j * n_tok, n_tok)
    rows = seq_len // GRID_W

    a = _mixer_a(uv_ref[...].astype(F32), vn_g_ref[...], vn_b_ref[...], ws_ref, bs_ref[...])
    mix_ref[:, :A_WIDTH] = a.astype(BF16)

    _fill_halo(zh_ref, z_ref, t0, n_tok, seq_len)
    bmix = _mixer_b(zh_ref, t0, n_tok, seq_len, wpool_ref[...], bscale_ref[...])
    mix_ref[:, A_WIDTH:A_WIDTH + B_WIDTH] = bmix.astype(BF16)

    win_row = jnp.clip(j * ROW_BLOCK - NA_ROWS // 2, 0, rows - KEY_ROWS)
    koff = pl.multiple_of(win_row * GRID_W, 2 * GRID_W)
    c_off = A_WIDTH + B_WIDTH
    for h in range(C_HEADS):
        hs = slice(h * HEAD_DIM, (h + 1) * HEAD_DIM)
        o = _attend(q_ref[:, hs],
                    kt_ref[hs, pl.ds(koff, KEY_ROWS * GRID_W)],
                    v_ref[pl.ds(koff, KEY_ROWS * GRID_W), hs],
                    bias_ref[h], kct_ref[hs, :], vc_ref[:, hs])
        mix_ref[:, c_off + h * HEAD_DIM:c_off + (h + 1) * HEAD_DIM] = o.astype(BF16)

    y = _dot(mix_ref[...], wout_ref[...])
    o_ref[...] = x_ref[...] + gate1 * y


def _mix(x, uv, z, q, k, v, kc, vc, bias, mods, vn_g, vn_b, ws, bs_full, wpool_bd, b_scale, w_out, *, l):
    bsz, seq_len, d = x.shape
    n_ctx = vc.shape[1]
    n_tok = ROW_BLOCK * GRID_W
    n_blocks = seq_len // n_tok
    assert ROW_BLOCK % 2 == 0 and (NA_ROWS // 2) % 2 == 0 and (seq_len // GRID_W - KEY_ROWS) % 2 == 0
    tok = lambda b, j: (b, j, 0)
    full = lambda b, j: (b, 0, 0)
    c2 = lambda b, j: (0, 0)
    c3 = lambda b, j: (0, 0, 0)

    def bias_map(b, j):
        kind = jnp.where(j == 0, 0, jnp.where(j == n_blocks - 1, 2, 1))
        return (l, kind, 0, 0, 0)

    return pl.pallas_call(
        functools.partial(_mix_kernel, seq_len=seq_len),
        grid=(bsz, n_blocks),
        in_specs=[
            pl.BlockSpec((None, n_tok, d), tok),
            pl.BlockSpec((None, n_tok, 2 * A_WIDTH), tok),
            pl.BlockSpec((None, seq_len, B_WIDTH), full),
            pl.BlockSpec((None, n_tok, C_WIDTH), tok),
            pl.BlockSpec((None, C_WIDTH, seq_len), full),
            pl.BlockSpec((None, seq_len, C_WIDTH), full),
            pl.BlockSpec((None, C_WIDTH, n_ctx), full),
            pl.BlockSpec((None, n_ctx, C_WIDTH), full),
            pl.BlockSpec((None, None, C_HEADS, n_tok, KEY_ROWS * GRID_W), bias_map),
            _mod_spec(l, d),
            _layer_spec(l, (1, A_WIDTH)),
            _layer_spec(l, (1, A_WIDTH)),
            _layer_spec(l, (A_GROUPS, CHUNK, CHUNK)),
            _layer_spec(l, (CHUNK, A_WIDTH)),
            _layer_spec(l, (B_WIDTH, B_WIDTH)),
            _layer_spec(l, (1, B_WIDTH)),
            _layer_spec(l, (d, d)),
        ],
        out_specs=pl.BlockSpec((None, n_tok, d), tok),
        out_shape=jax.ShapeDtypeStruct(x.shape, F32),
        scratch_shapes=[
            pltpu.VMEM((n_tok + 2 * POOL_HALO, B_WIDTH), F32),
            pltpu.VMEM((n_tok, d), BF16),
        ],
        compiler_params=_cparams(("arbitrary", "arbitrary")),
        name="mix",
    )(x, uv, z, q, k, v, kc, vc, bias, mods, vn_g, vn_b, ws, bs_full, wpool_bd, b_scale, w_out)


def _mix_ctx_kernel(x_ref, uv_ref, z_ref, q_ref, kt_ref, v_ref, mod_ref,
                    vn_g_ref, vn_b_ref, ws_ref, bs_ref, wpool_ref, bscale_ref, wout_ref,
                    o_ref, zh_ref, mix_ref):
    gate1 = _mod_chunks(mod_ref, True)[2]
    n_tok = x_ref.shape[0]
    a = _mixer_a(uv_ref[...].astype(F32), vn_g_ref[...], vn_b_ref[...], ws_ref, bs_ref[...])
    mix_ref[:, :A_WIDTH] = a.astype(BF16)

    zero = jnp.zeros((POOL_HALO, B_WIDTH), F32)
    zh_ref[0:POOL_HALO, :] = zero
    zh_ref[POOL_HALO:POOL_HALO + n_tok, :] = z_ref[...]
    zh_ref[POOL_HALO + n_tok:, :] = zero
    bmix = _mixer_b(zh_ref, 0, n_tok, n_tok, wpool_ref[...], bscale_ref[...])
    mix_ref[:, A_WIDTH:A_WIDTH + B_WIDTH] = bmix.astype(BF16)

    c_off = A_WIDTH + B_WIDTH
    for h in range(C_HEADS):
        hs = slice(h * HEAD_DIM, (h + 1) * HEAD_DIM)
        o = _attend(q_ref[:, hs], kt_ref[hs, :], v_ref[:, hs], None, None, None)
        mix_ref[:, c_off + h * HEAD_DIM:c_off + (h + 1) * HEAD_DIM] = o.astype(BF16)

    y = _dot(mix_ref[...], wout_ref[...])
    o_ref[...] = x_ref[...] + gate1 * y


def _mix_ctx(x, uv, z, q, k, v, mods, vn_g, vn_b, ws, bs_full, wpool_bd, b_scale, w_out, *, l):
    bsz, n_tok, d = x.shape
    full = lambda b: (b, 0, 0)
    c2 = lambda b: (0, 0)
    c3 = lambda b: (0, 0, 0)
    return pl.pallas_call(
        _mix_ctx_kernel,
        grid=(bsz,),
        in_specs=[
            pl.BlockSpec((None, n_tok, d), full),
            pl.BlockSpec((None, n_tok, 2 * A_WIDTH), full),
            pl.BlockSpec((None, n_tok, B_WIDTH), full),
            pl.BlockSpec((None, n_tok, C_WIDTH), full),
            pl.BlockSpec((None, C_WIDTH, n_tok), full),
            pl.BlockSpec((None, n_tok, C_WIDTH), full),
            _mod_spec(l, d),
            _layer_spec(l, (1, A_WIDTH)),
            _layer_spec(l, (1, A_WIDTH)),
            _layer_spec(l, (A_GROUPS, CHUNK, CHUNK)),
            _layer_spec(l, (CHUNK, A_WIDTH)),
            _layer_spec(l, (B_WIDTH, B_WIDTH)),
            _layer_spec(l, (1, B_WIDTH)),
            _layer_spec(l, (d, d)),
        ],
        out_specs=pl.BlockSpec((None, n_tok, d), full),
        out_shape=jax.ShapeDtypeStruct(x.shape, F32),
        scratch_shapes=[
            pltpu.VMEM((n_tok + 2 * POOL_HALO, B_WIDTH), F32),
            pltpu.VMEM((n_tok, d), BF16),
        ],
        compiler_params=_cparams(("arbitrary",)),
        name="mix_ctx",
    )(x, uv, z, q, k, v, mods, vn_g, vn_b, ws, bs_full, wpool_bd, b_scale, w_out)


def _ff_slice(f, tf):
    return pl.ds(pl.multiple_of(f * tf, tf), tf)


def _swiglu_tile(h_ref, acc_ref, act_ref, wg_ref, wu_ref, wd_ref, tf, prepare=None):
    nf = wg_ref.shape[1] // tf

    def up(f):
        h = h_ref[...]
        act = _silu(_dot(h, wg_ref[:, _ff_slice(f, tf)])) * _dot(h, wu_ref[:, _ff_slice(f, tf)])
        act_ref[f % 2] = act.astype(BF16)

    def down(f):
        acc_ref[...] += _dot(act_ref[f % 2], wd_ref[_ff_slice(f, tf), :])

    def body(f, carry):
        if prepare is not None:
            prepare(f)
        down(f - 1)
        up(f)
        return carry

    if prepare is not None:
        prepare(0)
    up(0)
    lax.fori_loop(1, nf, body, 0)
    down(nf - 1)


def _resident_spec(shape, index_map):
    return pl.BlockSpec(shape, index_map, pipeline_mode=pl.Buffered(1))


def _ffn_kernel(x_ref, mod_ref, g_ref, wg_ref, wu_ref, wd_ref, o_ref, h_ref, acc_ref, act_ref,
                *, is_ctx, tf):
    _, _, _, shift, scale, gate = _mod_chunks(mod_ref, is_ctx)
    h = _norm_mod(x_ref[...], g_ref[...], scale, shift)
    h_ref[...] = h.astype(BF16)
    acc_ref[...] = jnp.zeros_like(acc_ref)
    _swiglu_tile(h_ref, acc_ref, act_ref, wg_ref, wu_ref, wd_ref, tf)
    o_ref[...] = x_ref[...] + gate * acc_ref[...]


def _ffn(x, mods, g2, w_gate, w_up, w_down, *, l, fi, is_ctx, tm, tf):
    bsz, n_tok, d = x.shape
    d_ff = w_gate.shape[-1]
    tok = lambda b, i: (b, i, 0)
    layer = lambda b, i: (fi, 0, 0)
    return pl.pallas_call(
        functools.partial(_ffn_kernel, is_ctx=is_ctx, tf=tf),
        grid=(bsz, n_tok // tm),
        in_specs=[
            pl.BlockSpec((None, tm, d), tok),
            _mod_spec(l, d),
            _layer_spec(l, (1, d)),
            _resident_spec((None, d, d_ff), layer),
            _resident_spec((None, d, d_ff), layer),
            _resident_spec((None, d_ff, d), layer),
        ],
        out_specs=pl.BlockSpec((None, tm, d), tok),
        out_shape=jax.ShapeDtypeStruct(x.shape, F32),
        scratch_shapes=[pltpu.VMEM((tm, d), BF16), pltpu.VMEM((tm, d), F32),
                        pltpu.VMEM((2, tm, tf), BF16)],
        compiler_params=_cparams(("arbitrary", "arbitrary")),
        name="ffn_ctx" if is_ctx else "ffn",
    )(x, mods, g2, w_gate, w_up, w_down)


def _route_kernel(x_ref, mod_ref, g_ref, wr_ref, h_ref, meta_ref, cnt_ref, run_ref):
    @pl.when((pl.program_id(0) == 0) & (pl.program_id(1) == 0))
    def _():
        run_ref[...] = jnp.zeros_like(run_ref)

    _, _, _, shift, scale, _ = _mod_chunks(mod_ref, False)
    h = _norm_mod(x_ref[...], g_ref[...], scale, shift)
    h_ref[...] = h
    w = wr_ref[...]
    h_hi, w_hi = h.astype(BF16), w.astype(BF16)
    h_lo = (h - h_hi.astype(F32)).astype(BF16)
    w_lo = (w - w_hi.astype(F32)).astype(BF16)
    logits = _dot(h_hi, w_hi) + (_dot(h_lo, w_hi) + _dot(h_hi, w_lo))
    tm = logits.shape[0]
    lane = lax.broadcasted_iota(I32, logits.shape, 1)
    logits = jnp.where(lane < N_EXPERTS, logits, -jnp.inf)
    m1 = jnp.max(logits, axis=-1, keepdims=True)
    i1 = jnp.min(jnp.where(logits == m1, lane, META_LANES), axis=-1, keepdims=True)
    rest = jnp.where(lane == i1, -jnp.inf, logits)
    m2 = jnp.max(rest, axis=-1, keepdims=True)
    i2 = jnp.min(jnp.where(rest == m2, lane, META_LANES), axis=-1, keepdims=True)
    e2 = jnp.exp(m2 - m1)
    den = 1.0 + e2

    pick1 = lane == i1
    pick2 = lane == i2
    onehot = jnp.where(pick1, 1.0, 0.0) + jnp.where(pick2, 1.0, 0.0)
    earlier = (lax.broadcasted_iota(I32, (tm, tm), 0) > lax.broadcasted_iota(I32, (tm, tm), 1))
    before = run_ref[...] + _dot(jnp.where(earlier, 1.0, 0.0).astype(BF16), onehot.astype(BF16))
    r1 = jnp.sum(jnp.where(pick1, before, 0.0), axis=-1, keepdims=True)
    r2 = jnp.sum(jnp.where(pick2, before, 0.0), axis=-1, keepdims=True)
    run_ref[...] += jnp.sum(onehot, axis=0, keepdims=True)
    cnt_ref[...] = run_ref[...]

    meta = jnp.zeros(logits.shape, F32)
    for col, val in ((META_E, i1.astype(F32)), (META_E + 1, i2.astype(F32)),
                     (META_G, 1.0 / den), (META_G + 1, e2 / den),
                     (META_RANK, r1), (META_RANK + 1, r2)):
        meta = jnp.where(lane == col, val, meta)
    meta_ref[...] = meta


def _route(x, mods, g2, w_router_pad, *, l, fi, tm):
    bsz, n_tok, d = x.shape
    tok = lambda b, i: (b, i, 0)
    return pl.pallas_call(
        _route_kernel,
        grid=(bsz, n_tok // tm),
        in_specs=[
            pl.BlockSpec((None, tm, d), tok),
            _mod_spec(l, d),
            _layer_spec(l, (1, d)),
            _layer_spec(fi, (d, META_LANES)),
        ],
        out_specs=[
            pl.BlockSpec((None, tm, d), tok),
            pl.BlockSpec((None, tm, META_LANES), tok),
            pl.BlockSpec((1, META_LANES), lambda b, i: (0, 0)),
        ],
        out_shape=[
            jax.ShapeDtypeStruct((bsz, n_tok, d), F32),
            jax.ShapeDtypeStruct((bsz, n_tok, META_LANES), F32),
            jax.ShapeDtypeStruct((1, META_LANES), F32),
        ],
        scratch_shapes=[pltpu.VMEM((1, META_LANES), F32)],
        compiler_params=_cparams(("arbitrary", "arbitrary")),
        name="route",
    )(x, mods, g2, w_router_pad)


def _row_copy(src_ref, src_row, dst_ref, dst_row, sem):
    return pltpu.make_async_copy(src_ref.at[pl.ds(src_row, 1), :], dst_ref.at[pl.ds(dst_row, 1), :], sem)


def _dispatch_kernel(pos_ref, pad_start_ref, pad_count_ref, free_tile_ref, h_ref, xs_ref,
                     zero_ref, sem, zsem):
    tm = h_ref.shape[0]
    step = pl.program_id(0)

    @pl.when(step == 0)
    def _():
        zero_ref[...] = jnp.zeros_like(zero_ref)

        def tile_copy(t):
            return pltpu.make_async_copy(zero_ref, xs_ref.at[pl.ds(t * tm, tm), :], zsem)

        def fill_tile(t, carry):
            tile_copy(t).start()
            return carry

        def drain_tile(t, carry):
            tile_copy(t).wait()
            return carry

        n_tiles = xs_ref.shape[0] // tm
        lax.fori_loop(free_tile_ref[0], n_tiles, fill_tile, 0)
        lax.fori_loop(free_tile_ref[0], n_tiles, drain_tile, 0)
        for e in range(N_EXPERTS):
            start, count = pad_start_ref[e], pad_count_ref[e]
            head = (-start) & (SUBLANES - 1)
            chunks = [(i < head, _row_copy(zero_ref, 0, xs_ref, start + i, zsem))
                      for i in range(SUBLANES - 1)]
            body_start, body = start + head, count - head
            for bit in reversed(range(SUBLANES.bit_length() - 1, (MOE_TM - 1).bit_length())):
                size = 1 << bit
                assert size <= tm
                offset = pl.multiple_of(body_start + (body - (body & (2 * size - 1))), SUBLANES)
                copy = pltpu.make_async_copy(zero_ref.at[pl.ds(0, size), :],
                                             xs_ref.at[pl.ds(offset, size), :], zsem)
                chunks.append(((body & size) != 0, copy))
            for present, copy in chunks:
                pl.when(present)(copy.start)
            for present, copy in chunks:
                pl.when(present)(copy.wait)

    base = step * tm

    def send(r, carry):
        for s in range(TOP_K):
            _row_copy(h_ref, r, xs_ref, pos_ref[TOP_K * (base + r) + s], sem).start()
        return carry

    def done(r, carry):
        for s in range(TOP_K):
            _row_copy(h_ref, r, xs_ref, pos_ref[TOP_K * (base + r) + s], sem).wait()
        return carry

    lax.fori_loop(0, tm, send, 0, unroll=DMA_UNROLL)
    lax.fori_loop(0, tm, done, 0, unroll=DMA_UNROLL)


def _dispatch(h2, pos, pad_start, pad_count, free_tile, n_rows, *, tm):
    n_tok, d = h2.shape
    return pl.pallas_call(
        _dispatch_kernel,
        grid_spec=pltpu.PrefetchScalarGridSpec(
            num_scalar_prefetch=4,
            grid=(n_tok // tm,),
            in_specs=[pl.BlockSpec((tm, d), lambda i, *_: (i, 0))],
            out_specs=pl.BlockSpec(memory_space=pl.ANY),
            scratch_shapes=[pltpu.VMEM((tm, d), F32), pltpu.SemaphoreType.DMA(()),
                            pltpu.SemaphoreType.DMA(())],
        ),
        out_shape=jax.ShapeDtypeStruct((n_rows, d), F32),
        compiler_params=_cparams(("arbitrary",), has_side_effects=True),
        name="dispatch",
    )(pos, pad_start, pad_count, free_tile, h2)


def _gmoe_kernel(tile_expert_ref, tile_first_ref, n_used_ref, xs_ref, wg_hbm, wu_hbm, wd_hbm,
                 ys_ref, h_ref, act_ref, cg_ref, cu_ref, cd_ref, sg_ref, su_ref, sd_ref, sems,
                 *, fi, tf):
    t = pl.program_id(0)
    e = tile_expert_ref[t]
    nf = cg_ref.shape[1] // tf
    ys_ref[...] = jnp.zeros_like(ys_ref)

    def slice_copies(f):
        slot = f % 2
        return (pltpu.make_async_copy(wg_hbm.at[fi, e, :, _ff_slice(f, tf)], sg_ref.at[slot],
                                      sems.at[0, slot]),
                pltpu.make_async_copy(wu_hbm.at[fi, e, :, _ff_slice(f, tf)], su_ref.at[slot],
                                      sems.at[1, slot]),
                pltpu.make_async_copy(wd_hbm.at[fi, e, _ff_slice(f, tf), :], sd_ref.at[slot],
                                      sems.at[2, slot]))

    def start_slice(f):
        for copy in slice_copies(f):
            copy.start()

    def take_slice(f):
        @pl.when(f + 1 < nf)
        def _():
            start_slice(f + 1)

        for copy in slice_copies(f):
            copy.wait()
        slot = f % 2
        cg_ref[:, _ff_slice(f, tf)] = sg_ref[slot].astype(BF16)
        cu_ref[:, _ff_slice(f, tf)] = su_ref[slot].astype(BF16)
        cd_ref[_ff_slice(f, tf), :] = sd_ref[slot].astype(BF16)

    active = t < n_used_ref[0]
    fresh = tile_first_ref[t] == 1

    @pl.when(active)
    def _():
        h_ref[...] = xs_ref[...].astype(BF16)

    @pl.when(active & fresh)
    def _():
        start_slice(0)
        _swiglu_tile(h_ref, ys_ref, act_ref, cg_ref, cu_ref, cd_ref, tf, prepare=take_slice)

    @pl.when(active & jnp.logical_not(fresh))
    def _():
        _swiglu_tile(h_ref, ys_ref, act_ref, cg_ref, cu_ref, cd_ref, tf)


def _gmoe(xs, tile_expert, tile_first, n_used, w_gate, w_up, w_down, *, fi, tm, tf):
    n_rows, d = xs.shape
    d_ff = w_gate.shape[-1]

    def x_map(t, te, first, nu):
        return (jnp.maximum(jnp.minimum(t, nu[0] - 1), 0), 0)

    hbm = pl.BlockSpec(memory_space=pl.ANY)
    return pl.pallas_call(
        functools.partial(_gmoe_kernel, fi=fi, tf=tf),
        grid_spec=pltpu.PrefetchScalarGridSpec(
            num_scalar_prefetch=3,
            grid=(n_rows // tm,),
            in_specs=[pl.BlockSpec((tm, d), x_map), hbm, hbm, hbm],
            out_specs=pl.BlockSpec((tm, d), lambda t, te, first, nu: (t, 0)),
            scratch_shapes=[
                pltpu.VMEM((tm, d), BF16), pltpu.VMEM((2, tm, tf), BF16),
                pltpu.VMEM((d, d_ff), BF16), pltpu.VMEM((d, d_ff), BF16), pltpu.VMEM((d_ff, d), BF16),
                pltpu.VMEM((2, d, tf), F32), pltpu.VMEM((2, d, tf), F32), pltpu.VMEM((2, tf, d), F32),
                pltpu.SemaphoreType.DMA((3, 2)),
            ],
        ),
        out_shape=jax.ShapeDtypeStruct((n_rows, d), F32),
        compiler_params=_cparams(("arbitrary",)),
        name="grouped_moe",
    )(tile_expert, tile_first, n_used, xs, w_gate, w_up, w_down)


def _combine_kernel(pos_ref, x_ref, meta_ref, mod_ref, ys_ref, o_ref, buf_ref, sem):
    tm = x_ref.shape[0]
    base = (pl.program_id(0) * pl.num_programs(1) + pl.program_id(1)) * tm

    def fetch(r, carry):
        for s in range(TOP_K):
            _row_copy(ys_ref, pos_ref[TOP_K * (base + r) + s], buf_ref.at[s], r, sem).start()
        return carry

    def done(r, carry):
        for s in range(TOP_K):
            _row_copy(ys_ref, pos_ref[TOP_K * (base + r) + s], buf_ref.at[s], r, sem).wait()
        return carry

    lax.fori_loop(0, tm, fetch, 0, unroll=DMA_UNROLL)
    lax.fori_loop(0, tm, done, 0, unroll=DMA_UNROLL)
    meta = meta_ref[...]
    y = meta[:, META_G:META_G + 1] * buf_ref[0] + meta[:, META_G + 1:META_G + 2] * buf_ref[1]
    o_ref[...] = x_ref[...] + _mod_chunks(mod_ref, False)[5] * y


def _combine(x, meta, mods, ys, pos, *, l, tm):
    bsz, n_tok, d = x.shape
    tok = lambda b, i, *_: (b, i, 0)
    return pl.pallas_call(
        _combine_kernel,
        grid_spec=pltpu.PrefetchScalarGridSpec(
            num_scalar_prefetch=1,
            grid=(bsz, n_tok // tm),
            in_specs=[
                pl.BlockSpec((None, tm, d), tok),
                pl.BlockSpec((None, tm, META_LANES), tok),
                _mod_spec(l, d),
                pl.BlockSpec(memory_space=pl.ANY),
            ],
            out_specs=pl.BlockSpec((None, tm, d), tok),
            scratch_shapes=[pltpu.VMEM((TOP_K, tm, d), F32), pltpu.SemaphoreType.DMA(())],
        ),
        out_shape=jax.ShapeDtypeStruct(x.shape, F32),
        compiler_params=_cparams(("arbitrary", "arbitrary")),
        name="combine",
    )(pos, x, meta, mods, ys)


def _moe(x, mods, g2, w_router, w_gate, w_up, w_down, *, l, fi):
    bsz, n_tok, d = x.shape
    tm = MOE_TM
    n_all = bsz * n_tok
    n_tiles = TOP_K * n_all // tm + N_EXPERTS
    wr = jnp.pad(w_router, ((0, 0), (0, 0), (0, META_LANES - N_EXPERTS)))
    h2, meta, counts = _route(x, mods, g2, wr, l=l, fi=fi, tm=ROUTE_TM)

    meta2 = meta.reshape(n_all, META_LANES)
    expert = meta2[:, META_E:META_E + TOP_K].astype(I32)
    rank = meta2[:, META_RANK:META_RANK + TOP_K].astype(I32)
    cnt = counts[0, :N_EXPERTS].astype(I32)
    tiles = (cnt + tm - 1) // tm
    tile_end = jnp.cumsum(tiles)
    row_start = (tile_end - tiles) * tm
    expert_ids = jnp.arange(N_EXPERTS, dtype=I32)
    pos = (rank + jnp.sum(jnp.where(expert[..., None] == expert_ids, row_start, 0), axis=-1)).reshape(-1)
    n_used = tile_end[-1:]
    tile_ids = jnp.arange(n_tiles, dtype=I32)
    tile_expert = jnp.sum(tile_ids[:, None] >= tile_end[None, :], axis=1).astype(I32)
    last_expert = jnp.max(jnp.where(tiles > 0, expert_ids, 0))
    tile_expert = jnp.minimum(tile_expert, last_expert)
    tile_first = jnp.any((tile_ids[:, None] == (tile_end - tiles)[None, :]) & (tiles > 0)[None, :],
                         axis=1).astype(I32)
    pad_start = row_start + cnt
    pad_count = tiles * tm - cnt

    free_tile = n_used * (tm // ROUTE_TM)
    xs = _dispatch(h2.reshape(n_all, d), pos, pad_start, pad_count, free_tile, n_tiles * tm, tm=ROUTE_TM)
    ys = _gmoe(xs, tile_expert, tile_first, n_used, w_gate, w_up, w_down, fi=fi, tm=tm, tf=512)
    return _combine(x, meta, mods, ys, pos, l=l, tm=ROUTE_TM)


N_ROW_OFF = 2 * NA_ROWS - 1
N_COL_OFF = 2 * NA_COLS - 1


def _bias_cells(rows):
    n_blocks = rows // ROW_BLOCK
    row_off = np.zeros((3, ROW_BLOCK, KEY_ROWS), np.int64)
    row_in = np.zeros((3, ROW_BLOCK, KEY_ROWS), bool)
    for kind, blk in enumerate((0, 1, n_blocks - 1)):
        win = int(np.clip(blk * ROW_BLOCK - NA_ROWS // 2, 0, rows - KEY_ROWS))
        r = blk * ROW_BLOCK + np.arange(ROW_BLOCK)[:, None]
        kr = win + np.arange(KEY_ROWS)[None, :]
        first = np.clip(r - NA_ROWS // 2, 0, rows - NA_ROWS)
        row_in[kind] = (kr >= first) & (kr < first + NA_ROWS)
        row_off[kind] = np.clip(kr - r + NA_ROWS - 1, 0, N_ROW_OFF - 1)
    return row_off, row_in


def _bias_kernel(rpb_ref, o_ref, toep_ref, *, rows):
    l, h, kind = pl.program_id(0), pl.program_id(1), pl.program_id(2)
    shape = (GRID_W, 2 * GRID_W)
    lane = lax.broadcasted_iota(I32, shape, 1)

    @pl.when(kind == 0)
    def _():
        qc = lax.broadcasted_iota(I32, shape, 0)
        kc = lane % GRID_W
        start = jnp.clip(qc - NA_COLS // 2, 0, GRID_W - NA_COLS)
        col_in = (kc >= start) & (kc < start + NA_COLS)
        col_off = jnp.clip(kc - qc, -(NA_COLS - 1), NA_COLS - 1) + (NA_COLS - 1)
        base = (l * C_HEADS + h) * (N_ROW_OFF * N_COL_OFF)
        for dr in range(N_ROW_OFF):
            t = jnp.full(shape, NEG_INF, F32)
            for dc in range(N_COL_OFF):
                t = jnp.where(col_off == dc, rpb_ref[base + dr * N_COL_OFF + dc], t)
            toep_ref[dr] = jnp.where(col_in, t, NEG_INF)

    row_off, row_in = _bias_cells(rows)
    masked = jnp.full(shape, NEG_INF, F32)
    for k in range(3):
        @pl.when(kind == k)
        def _():
            for i in range(ROW_BLOCK):
                for jp in range(KEY_ROWS // 2):
                    halves = [toep_ref[int(row_off[k, i, jj])] if row_in[k, i, jj] else masked
                              for jj in (2 * jp, 2 * jp + 1)]
                    o_ref[i * GRID_W:(i + 1) * GRID_W, jp * 2 * GRID_W:(jp + 1) * 2 * GRID_W] = (
                        jnp.where(lane < GRID_W, halves[0], halves[1]))


def _attention_bias(rpb, rows):
    depth = rpb.shape[0]
    assert KEY_ROWS % 2 == 0
    return pl.pallas_call(
        functools.partial(_bias_kernel, rows=rows),
        grid=(depth, C_HEADS, 3),
        in_specs=[pl.BlockSpec(memory_space=pltpu.SMEM)],
        out_specs=pl.BlockSpec((None, None, None, ROW_BLOCK * GRID_W, KEY_ROWS * GRID_W),
                               lambda l, h, k: (l, k, h, 0, 0)),
        out_shape=jax.ShapeDtypeStruct(
            (depth, 3, C_HEADS, ROW_BLOCK * GRID_W, KEY_ROWS * GRID_W), F32),
        scratch_shapes=[pltpu.VMEM((N_ROW_OFF, GRID_W, 2 * GRID_W), F32)],
        compiler_params=_cparams(("arbitrary", "arbitrary", "arbitrary")),
        name="attention_bias",
    )(rpb.astype(F32).reshape(-1))


def _block_diag(blocks):
    g, m, n = blocks.shape[-3:]
    eye = jnp.eye(g, dtype=blocks.dtype)
    out = blocks[..., :, :, None, :] * eye[:, None, :, None]
    return out.reshape(blocks.shape[:-3] + (g * m, g * n))


def kernel(x, c, ctx, c_ctx, w_mod, b_mod, norm1_g, norm2_g, w_in, w_out, a_vn_g, a_vn_b, a_ws, a_bs,
           b_wpool, b_scale, c_qn_g, c_kn_g, c_rpb, ffn_w_gate, ffn_w_up, ffn_w_down,
           moe_w_router, moe_w_gate, moe_w_up, moe_w_down):
    bsz, seq_len, d = x.shape
    depth = w_mod.shape[0]
    rows = seq_len // GRID_W
    assert bsz <= CTX_MOD_ROW

    cond = jnp.concatenate([c, jnp.zeros((CTX_MOD_ROW - bsz, d), F32), c_ctx[None, :],
                            jnp.zeros((MOD_ROWS - CTX_MOD_ROW - 1, d), F32)], axis=0)
    mods = _modulation(cond, w_mod, b_mod)
    hsum = jnp.asarray(np.kron(np.eye(C_HEADS), np.full((HEAD_DIM, HEAD_DIM), 1.0 / HEAD_DIM)), BF16)

    g1 = norm1_g.reshape(depth, 1, d)
    g2 = norm2_g.reshape(depth, 1, d)
    w_in_b = w_in.astype(BF16)
    qg = (jnp.tile(c_qn_g, (1, C_HEADS)) * ATTN_SCALE).reshape(depth, 1, C_WIDTH)
    kg = jnp.tile(c_kn_g, (1, C_HEADS)).reshape(depth, 1, C_WIDTH)
    mix_w = (a_vn_g.reshape(depth, 1, A_WIDTH), a_vn_b.reshape(depth, 1, A_WIDTH),
             a_ws.astype(BF16),
             jnp.repeat(jnp.swapaxes(a_bs, 1, 2), HEAD_DIM, axis=2),
             _block_diag(b_wpool).astype(BF16),
             b_scale.reshape(depth, 1, B_WIDTH),
             w_out.astype(BF16))
    bias = _attention_bias(c_rpb, rows)

    xc = ctx
    for l in range(depth):
        last = l == depth - 1
        fi = l // 2
        uv_c, z_c, q_c, k_c, v_c = _in_proj(xc, mods, g1, w_in_b, hsum, qg, kg,
                                            l=l, is_ctx=True, tm=xc.shape[1])
        uv, z, q, k, v = _in_proj(x, mods, g1, w_in_b, hsum, qg, kg, l=l, is_ctx=False, tm=1024)
        x = _mix(x, uv, z, q, k, v, k_c, v_c, bias, mods, *mix_w, l=l)

        ffn_w = (ffn_w_gate.astype(BF16), ffn_w_up.astype(BF16), ffn_w_down.astype(BF16))
        if l % 2 == 0:
            x = _ffn(x, mods, g2, *ffn_w, l=l, fi=fi, is_ctx=False, tm=1024, tf=256)
        else:
            x = _moe(x, mods, g2, moe_w_router, moe_w_gate, moe_w_up, moe_w_down, l=l, fi=fi)

        if not last:
            xc = _mix_ctx(xc, uv_c, z_c, q_c, k_c, v_c, mods, *mix_w, l=l)
            n_ctx = xc.shape[1]
            if l % 2 == 0:
                xc = _ffn(xc.reshape(1, bsz * n_ctx, d), mods, g2, *ffn_w, l=l, fi=fi,
                          is_ctx=True, tm=bsz * n_ctx, tf=256).reshape(bsz, n_ctx, d)
            else:
                raise NotImplementedError("a context-stream MoE layer only occurs for depth > 2")
    return x
```
